```python
import jax, jax.numpy as jnp
from jax import lax
import numpy as np

D_MODEL = 2048
BATCH = 4
SEQ = 4096
DEPTH = 2

GRID_W = 64
CTX_LEN = 256
MLA_HEADS = 8
MLA_Q_RANK = 512
MLA_KV_RANK = 512
MLA_NOPE = 128
MLA_ROPE = 64
MLA_V = 128
SWA_HEADS = 8
SWA_KV_HEADS = 2
SWA_HEAD_DIM = 64
SWA_WINDOW = 128
SWA_BLOCK = 128
CONV_CH = 512
CONV_K = 31
D_FF = 4 * D_MODEL

Q_BLOCK = 128
ROPE_THETA = 10000.0
EPS = 1e-6
NEG_INF = -1e30
MLA_SCALE = (MLA_NOPE + MLA_ROPE) ** -0.5
SWA_SCALE = SWA_HEAD_DIM ** -0.5
SWA_GROUP = SWA_HEADS // SWA_KV_HEADS

MLA_OUT = MLA_HEADS * MLA_V
SWA_OUT = SWA_HEADS * SWA_HEAD_DIM
MIX_WIDTH = MLA_OUT + SWA_OUT + CONV_CH
IN_SIZES = (MLA_Q_RANK, MLA_KV_RANK, MLA_ROPE,
            SWA_HEADS * SWA_HEAD_DIM, SWA_KV_HEADS * SWA_HEAD_DIM, SWA_KV_HEADS * SWA_HEAD_DIM,
            2 * CONV_CH)
IN_WIDTH = sum(IN_SIZES)
IN_OFFSETS = tuple(sum(IN_SIZES[:i + 1]) for i in range(len(IN_SIZES) - 1))

kernel_name = "hybrid_parallel_groups_dit_block"


def rms_norm(x, g):
    xf = x.astype(jnp.float32)
    y = xf * lax.rsqrt(jnp.mean(xf * xf, axis=-1, keepdims=True) + EPS)
    return (y * g.astype(jnp.float32)).astype(x.dtype)


def layer_norm(x, g, b):
    xf = x.astype(jnp.float32)
    mu = jnp.mean(xf, axis=-1, keepdims=True)
    xc = xf - mu
    y = xc * lax.rsqrt(jnp.mean(xc * xc, axis=-1, keepdims=True) + EPS)
    return (y * g.astype(jnp.float32) + b.astype(jnp.float32)).astype(x.dtype)


def modulate(x, g, shift, scale):
    return rms_norm(x, g) * (1 + scale) + shift


def axial_rope(rows, rot_dim):
    row = jnp.repeat(jnp.arange(rows, dtype=jnp.float32), GRID_W)
    col = jnp.tile(jnp.arange(GRID_W, dtype=jnp.float32), rows)
    n_freq = rot_dim // 4
    inv_freq = ROPE_THETA ** (-jnp.arange(n_freq, dtype=jnp.float32) / n_freq)
    ang = jnp.stack([row[:, None] * inv_freq, col[:, None] * inv_freq], axis=1)
    return jnp.cos(ang), jnp.sin(ang)


def rope_2d(x, cos, sin):
    b_, s_, h_, r_ = x.shape
    xr = x.reshape(b_, s_, h_, 2, 2, r_ // 4)
    x1, x2 = xr[..., 0, :], xr[..., 1, :]
    c = cos[None, :, None].astype(x.dtype)
    s = sin[None, :, None].astype(x.dtype)
    y = jnp.stack([x1 * c - x2 * s, x2 * c + x1 * s], axis=-2)
    return y.reshape(x.shape)


def blocked_attention(q, k, v, scale):
    b_, s_, h_, d_ = q.shape
    nb = s_ // Q_BLOCK
    qb = q.reshape(b_, nb, Q_BLOCK, h_, d_).transpose(1, 0, 2, 3, 4)

    def one_block(q_blk):
        s = jnp.einsum('bqhd,bkhd->bhqk', q_blk, k).astype(jnp.float32) * scale
        p = jax.nn.softmax(s, axis=-1).astype(v.dtype)
        return jnp.einsum('bhqk,bkhd->bqhd', p, v)

    out = lax.map(one_block, qb)
    return out.transpose(1, 0, 2, 3, 4).reshape(b_, s_, h_, v.shape[-1])


def mla_q(a_q, q_norm, w_uq, rope):
    q = jnp.einsum('bsr,rhd->bshd', rms_norm(a_q, q_norm), w_uq)
    q_nope, q_pe = q[..., :MLA_NOPE], q[..., MLA_NOPE:]
    if rope is not None:
        q_pe = rope_2d(q_pe, *rope)
    return jnp.concatenate([q_nope, q_pe], axis=-1)


def mla_kv(a_kv, a_kr, kv_norm, w_ukv, rope):
    kv = jnp.einsum('bsr,rhd->bshd', rms_norm(a_kv, kv_norm), w_ukv)
    k_nope, v = kv[..., :MLA_NOPE], kv[..., MLA_NOPE:]
    k_pe = a_kr[:, :, None, :]
    if rope is not None:
        k_pe = rope_2d(k_pe, *rope)
    k_pe = jnp.broadcast_to(k_pe, k_nope.shape[:-1] + (MLA_ROPE,))
    return jnp.concatenate([k_nope, k_pe], axis=-1), v


def swa_latent(q, k, v, kc, vc, sink):
    b_, s_, h_, dh = q.shape
    nb = s_ // SWA_BLOCK
    win = 3 * SWA_BLOCK
    pad = ((0, 0), (SWA_BLOCK, SWA_BLOCK), (0, 0), (0, 0))
    kp = jnp.pad(k, pad).reshape(b_, nb + 2, SWA_BLOCK, SWA_KV_HEADS, dh)
    vp = jnp.pad(v, pad).reshape(b_, nb + 2, SWA_BLOCK, SWA_KV_HEADS, dh)
    kw = jnp.concatenate([kp[:, :-2], kp[:, 1:-1], kp[:, 2:]], axis=2)
    vw = jnp.concatenate([vp[:, :-2], vp[:, 1:-1], vp[:, 2:]], axis=2)
    qb = q.reshape(b_, nb, SWA_BLOCK, SWA_KV_HEADS, SWA_GROUP, dh)
    s_loc = jnp.einsum('bnqkgd,bnwkd->bnkgqw', qb, kw).astype(jnp.float32) * SWA_SCALE
    s_ctx = jnp.einsum('bnqkgd,bckd->bnkgqc', qb, kc).astype(jnp.float32) * SWA_SCALE
    qi = jnp.arange(SWA_BLOCK)[:, None]
    wi = jnp.arange(win)[None, :]
    kpos = jnp.arange(nb)[:, None, None] * SWA_BLOCK - SWA_BLOCK + wi
    valid = (jnp.abs(wi - SWA_BLOCK - qi) <= SWA_WINDOW)[None] & (kpos >= 0) & (kpos < s_)
    s_loc = jnp.where(valid[None, :, None, None], s_loc, NEG_INF)
    s_sink = jnp.broadcast_to(
        sink.reshape(SWA_KV_HEADS, SWA_GROUP)[None, None, :, :, None, None].astype(jnp.float32),
        s_loc.shape[:-1] + (1,))
    p = jax.nn.softmax(jnp.concatenate([s_loc, s_ctx, s_sink], axis=-1), axis=-1).astype(v.dtype)
    n_ctx = kc.shape[1]
    out = (jnp.einsum('bnkgqw,bnwkd->bnqkgd', p[..., :win], vw)
           + jnp.einsum('bnkgqc,bckd->bnqkgd', p[..., win:win + n_ctx], vc))
    return out.reshape(b_, s_, h_, dh)


def swa_context(qc, kc, vc, sink):
    b_, l_, h_, dh = qc.shape
    qg = qc.reshape(b_, l_, SWA_KV_HEADS, SWA_GROUP, dh)
    s = jnp.einsum('bqkgd,bckd->bkgqc', qg, kc).astype(jnp.float32) * SWA_SCALE
    s_sink = jnp.broadcast_to(
        sink.reshape(SWA_KV_HEADS, SWA_GROUP)[None, :, :, None, None].astype(jnp.float32),
        s.shape[:-1] + (1,))
    p = jax.nn.softmax(jnp.concatenate([s, s_sink], axis=-1), axis=-1).astype(vc.dtype)
    out = jnp.einsum('bkgqc,bckd->bqkgd', p[..., :l_], vc)
    return out.reshape(b_, l_, h_, dh)


def conformer_conv(u, conv_w, conv_b, ln_g, ln_b):
    a, g = jnp.split(u, 2, axis=-1)
    h = a * jax.nn.sigmoid(g)
    h = lax.conv_general_dilated(
        h, conv_w, window_strides=(1,), padding=((CONV_K // 2, CONV_K // 2),),
        dimension_numbers=('NWC', 'WIO', 'NWC'), feature_group_count=CONV_CH) + conv_b
    return jax.nn.silu(layer_norm(h, ln_g, ln_b))


def merge_groups(out_a, out_b, out_c, out_norm, w_out):
    b_, s_ = out_c.shape[:2]
    y = jnp.concatenate([
        rms_norm(out_a.reshape(b_, s_, MLA_OUT), out_norm[:MLA_OUT]),
        rms_norm(out_b.reshape(b_, s_, SWA_OUT), out_norm[MLA_OUT:MLA_OUT + SWA_OUT]),
        rms_norm(out_c, out_norm[MLA_OUT + SWA_OUT:]),
    ], axis=-1)
    return y @ w_out


def token_mixer(h, hc, w_in, q_norm, w_uq, kv_norm, w_ukv, sink, conv_w, conv_b,
                ln_g, ln_b, out_norm, w_out, rope_a, rope_b, with_ctx_out):
    b_, s_, _ = h.shape
    l_ = hc.shape[1]
    a_q, a_kv, a_kr, b_q, b_k, b_v, c_in = jnp.split(h @ w_in, IN_OFFSETS, axis=-1)
    ac_q, ac_kv, ac_kr, bc_q, bc_k, bc_v, cc_in = jnp.split(hc @ w_in, IN_OFFSETS, axis=-1)

    k_a, v_a = mla_kv(a_kv, a_kr, kv_norm, w_ukv, rope_a)
    kc_a, vc_a = mla_kv(ac_kv, ac_kr, kv_norm, w_ukv, None)
    q_a = mla_q(a_q, q_norm, w_uq, rope_a)
    out_a = blocked_attention(q_a, jnp.concatenate([k_a, kc_a], axis=1),
                              jnp.concatenate([v_a, vc_a], axis=1), MLA_SCALE)

    q_b = rope_2d(b_q.reshape(b_, s_, SWA_HEADS, SWA_HEAD_DIM), *rope_b)
    k_b = rope_2d(b_k.reshape(b_, s_, SWA_KV_HEADS, SWA_HEAD_DIM), *rope_b)
    v_b = b_v.reshape(b_, s_, SWA_KV_HEADS, SWA_HEAD_DIM)
    kc_b = bc_k.reshape(b_, l_, SWA_KV_HEADS, SWA_HEAD_DIM)
    vc_b = bc_v.reshape(b_, l_, SWA_KV_HEADS, SWA_HEAD_DIM)
    out_b = swa_latent(q_b, k_b, v_b, kc_b, vc_b, sink)

    out_c = conformer_conv(c_in, conv_w, conv_b, ln_g, ln_b)

    y = merge_groups(out_a, out_b, out_c, out_norm, w_out)
    if not with_ctx_out:
        return y, None
    outc_a = blocked_attention(mla_q(ac_q, q_norm, w_uq, None), kc_a, vc_a, MLA_SCALE)
    outc_b = swa_context(bc_q.reshape(b_, l_, SWA_HEADS, SWA_HEAD_DIM), kc_b, vc_b, sink)
    outc_c = conformer_conv(cc_in, conv_w, conv_b, ln_g, ln_b)
    yc = merge_groups(outc_a, outc_b, outc_c, out_norm, w_out)
    return y, yc


def sq_relu_mlp(h, w1, w2):
    return jnp.square(jax.nn.relu(h @ w1)) @ w2


def setup_inputs(seed: int = 0) -> dict:
    key = jax.random.key(seed)
    ks = jax.random.split(key, 24)
    f32 = jnp.float32
    L, D = DEPTH, D_MODEL

    def dense(k, shape, fan_in, mult=1.0):
        return jax.random.normal(k, shape, f32) * (mult * fan_in ** -0.5)

    def gain(k, shape):
        return 1.0 + 0.05 * jax.random.normal(k, shape, f32)

    def small(k, shape, s=0.02):
        return s * jax.random.normal(k, shape, f32)

    return {
        "x": jax.random.normal(ks[0], (BATCH, SEQ, D), f32),
        "c": jax.random.normal(ks[1], (BATCH, D), f32),
        "ctx": jax.random.normal(ks[2], (BATCH, CTX_LEN, D), f32),
        "c_ctx": jax.random.normal(ks[3], (D,), f32),
        "ada_w": dense(ks[4], (L, D, 6 * D), D, 0.5),
        "ada_b": small(ks[5], (L, 6 * D)),
        "norm_mix": gain(ks[6], (L, D)),
        "norm_mlp": gain(ks[7], (L, D)),
        "w_in": dense(ks[8], (L, D, IN_WIDTH), D),
        "mla_q_norm": gain(ks[9], (L, MLA_Q_RANK)),
        "mla_w_uq": dense(ks[10], (L, MLA_Q_RANK, MLA_HEADS, MLA_NOPE + MLA_ROPE), MLA_Q_RANK),
        "mla_kv_norm": gain(ks[11], (L, MLA_KV_RANK)),
        "mla_w_ukv": dense(ks[12], (L, MLA_KV_RANK, MLA_HEADS, MLA_NOPE + MLA_V), MLA_KV_RANK),
        "swa_sink": small(ks[13], (L, SWA_HEADS), 0.5),
        "conv_w": dense(ks[14], (L, CONV_K, 1, CONV_CH), CONV_K),
        "conv_b": small(ks[15], (L, CONV_CH)),
        "conv_ln_g": gain(ks[16], (L, CONV_CH)),
        "conv_ln_b": small(ks[17], (L, CONV_CH)),
        "out_norm": gain(ks[18], (L, MIX_WIDTH)),
        "w_out": dense(ks[19], (L, MIX_WIDTH, D), MIX_WIDTH),
        "mlp_w1": dense(ks[20], (L, D, D_FF), D),
        "mlp_w2": dense(ks[21], (L, D_FF, D), D_FF),
        "final_norm": gain(ks[22], (D,)),
    }


def reference(x, c, ctx, c_ctx, ada_w, ada_b, norm_mix, norm_mlp, w_in, mla_q_norm,
              mla_w_uq, mla_kv_norm, mla_w_ukv, swa_sink, conv_w, conv_b, conv_ln_g,
              conv_ln_b, out_norm, w_out, mlp_w1, mlp_w2, final_norm):
    n_tok = x.shape[1]
    rows = n_tok // GRID_W
    rope_a = axial_rope(rows, MLA_ROPE)
    rope_b = axial_rope(rows, SWA_HEAD_DIM)
    silu_c = jax.nn.silu(c)
    silu_cc = jax.nn.silu(c_ctx)
    xc = ctx
    for l in range(DEPTH):
        update_ctx = l < DEPTH - 1
        mod = silu_c @ ada_w[l] + ada_b[l]
        mod_c = silu_cc @ ada_w[l] + ada_b[l]
        sh1, sc1, g1, sh2, sc2, g2 = jnp.split(mod[:, None, :], 6, axis=-1)
        sh1c, sc1c, g1c, sh2c, sc2c, g2c = jnp.split(mod_c, 6, axis=-1)
        h = modulate(x, norm_mix[l], sh1, sc1)
        hc = modulate(xc, norm_mix[l], sh1c, sc1c)
        y, yc = token_mixer(h, hc, w_in[l], mla_q_norm[l], mla_w_uq[l], mla_kv_norm[l],
                            mla_w_ukv[l], swa_sink[l], conv_w[l], conv_b[l], conv_ln_g[l],
                            conv_ln_b[l], out_norm[l], w_out[l], rope_a, rope_b, update_ctx)
        x = x + g1 * y
        x = x + g2 * sq_relu_mlp(modulate(x, norm_mlp[l], sh2, sc2), mlp_w1[l], mlp_w2[l])
        if update_ctx:
            xc = xc + g1c * yc
            xc = xc + g2c * sq_relu_mlp(modulate(xc, norm_mlp[l], sh2c, sc2c), mlp_w1[l], mlp_w2[l])
    return rms_norm(x, final_norm)
```

```python
import functools

import jax
import jax.numpy as jnp
from jax import lax
from jax.experimental import pallas as pl
from jax.experimental.pallas import tpu as pltpu

F32 = jnp.float32
BF16 = jnp.bfloat16

EPS = 1e-6
NEG_INF = -1e30
GRID_W = 64
ROPE_THETA = 10000.0

MLA_HEADS = 8
MLA_RANK = 512
MLA_NOPE = 128
MLA_ROPE = 64
MLA_V = 128
MLA_QK_PAD = 256
MLA_SCALE = (MLA_NOPE + MLA_ROPE) ** -0.5
SWA_HEADS = 8
SWA_KV_HEADS = 2
SWA_GROUP = SWA_HEADS // SWA_KV_HEADS
SWA_DH = 64
SWA_WINDOW = 128
SWA_BLOCK = 128
SWA_SCALE = SWA_DH ** -0.5
CONV_CH = 512
CONV_K = 31
CONV_HALO = 16

OFF_AQ = 0
OFF_AKV = 512
OFF_KR = 1024
OFF_BQ = 1152
OFF_BK = 1664
OFF_BV = 1792
OFF_C = 1920
IN_PAD_WIDTH = 2944

ROW_TILE = 256
MLP_ROWS = 512
MLP_FF_TILE = 1024
MLA_Q_TILE = 256
MLA_K_TILE = 512
ADA_N_TILE = 1024
VMEM_LIMIT = 56 * 1024 * 1024


def _cparams(sem):
    return pltpu.CompilerParams(dimension_semantics=sem, vmem_limit_bytes=VMEM_LIMIT)


def _const_spec(shape):
    nd = len(shape)
    return pl.BlockSpec(shape, lambda *_: (0,) * nd, pipeline_mode=pl.Buffered(1))


def _rms(x, g):
    return x * lax.rsqrt(jnp.mean(x * x, axis=-1, keepdims=True) + EPS) * g


def _dot(a, b):
    return jnp.dot(a, b, preferred_element_type=F32)


def _dot_nt(a, b):
    return lax.dot_general(a, b, (((1,), (1,)), ((), ())), preferred_element_type=F32)


def _ada_kernel(c_ref, w_ref, b_ref, o_ref):
    c = c_ref[...]
    s = c * jax.nn.sigmoid(c)
    o_ref[0] = _dot(s.astype(BF16), w_ref[0].astype(BF16)) + b_ref[0]


def _ada(cvec, ada_w, ada_b):
    n_layers, d, n = ada_w.shape
    rows = cvec.shape[0]
    tn = ADA_N_TILE
    return pl.pallas_call(
        _ada_kernel,
        grid=(n_layers, n // tn),
        in_specs=[
            pl.BlockSpec((rows, d), lambda l, j: (0, 0)),
            pl.BlockSpec((1, d, tn), lambda l, j: (l, 0, j)),
            pl.BlockSpec((1, 1, tn), lambda l, j: (l, 0, j)),
        ],
        out_specs=pl.BlockSpec((1, rows, tn), lambda l, j: (l, 0, j)),
        out_shape=jax.ShapeDtypeStruct((n_layers, rows, n), F32),
        compiler_params=_cparams(("arbitrary", "arbitrary")),
        name="ada",
    )(cvec, ada_w, ada_b.reshape(n_layers, 1, n))


def _rope128(t, cos, sin):
    lane = lax.broadcasted_iota(jnp.int32, t.shape, 1)
    up = pltpu.roll(t, 128 - 16, 1)
    dn = pltpu.roll(t, 16, 1)
    sw = jnp.where((lane & 16) == 0, up, dn)
    return t * cos + sw * sin


def _premix_kernel(*refs, use_rope):
    if use_rope:
        (x_ref, mod_ref, g_ref, win_ref, qn_ref, wuq_ref, kvn_ref, wukv_ref, cos_ref, sin_ref,
         qa_ref, ka_ref, va_ref, bq_ref, bk_ref, bv_ref, glu_ref) = refs
        cos = cos_ref[...]
        sin = sin_ref[...]
        rope = lambda t: _rope128(t, cos, sin)
    else:
        (x_ref, mod_ref, g_ref, win_ref, qn_ref, wuq_ref, kvn_ref, wukv_ref,
         qa_ref, ka_ref, va_ref, bq_ref, bk_ref, bv_ref, glu_ref) = refs
        rope = lambda t: t

    m = mod_ref[0]
    h = _rms(x_ref[0], g_ref[...]) * (1.0 + m[1:2]) + m[0:1]
    p = _dot(h.astype(BF16), win_ref[...])

    qn = _rms(p[:, OFF_AQ:OFF_AQ + MLA_RANK], qn_ref[...])
    q = _dot(qn.astype(BF16), wuq_ref[...])
    for hd in range(MLA_HEADS):
        c0 = hd * MLA_QK_PAD
        qa_ref[0, hd, :, 0:128] = (q[:, c0:c0 + 128] * MLA_SCALE).astype(BF16)
        qa_ref[0, hd, :, 128:256] = (rope(q[:, c0 + 128:c0 + 256]) * MLA_SCALE).astype(BF16)

    kvn = _rms(p[:, OFF_AKV:OFF_AKV + MLA_RANK], kvn_ref[...])
    kv = _dot(kvn.astype(BF16), wukv_ref[...])
    kpe = rope(p[:, OFF_KR:OFF_KR + 128]).astype(BF16)
    for hd in range(MLA_HEADS):
        c0 = hd * (MLA_NOPE + MLA_V)
        ka_ref[0, hd, :, 0:128] = kv[:, c0:c0 + 128].astype(BF16)
        ka_ref[0, hd, :, 128:256] = kpe
        va_ref[0, hd] = kv[:, c0 + 128:c0 + 256].astype(BF16)

    for t in range(SWA_HEADS * SWA_DH // 128):
        c0 = OFF_BQ + t * 128
        bq_ref[0, :, t * 128:(t + 1) * 128] = (rope(p[:, c0:c0 + 128]) * SWA_SCALE).astype(BF16)
    bk_ref[0] = rope(p[:, OFF_BK:OFF_BK + 128]).astype(BF16)
    bv_ref[0] = p[:, OFF_BV:OFF_BV + 128].astype(BF16)

    glu_ref[0] = p[:, OFF_C:OFF_C + CONV_CH] * jax.nn.sigmoid(p[:, OFF_C + CONV_CH:OFF_C + 2 * CONV_CH])


def _premix(x, mod, g, w_in, qn, w_uq, kvn, w_ukv, rope_tabs):
    b, t, d = x.shape
    tm = ROW_TILE
    use_rope = rope_tabs is not None
    in_specs = [
        pl.BlockSpec((1, tm, d), lambda i, j: (i, j, 0)),
        pl.BlockSpec((1, 6, d), lambda i, j: (i, 0, 0)),
        _const_spec(g.shape),
        _const_spec(w_in.shape),
        _const_spec(qn.shape),
        _const_spec(w_uq.shape),
        _const_spec(kvn.shape),
        _const_spec(w_ukv.shape),
    ]
    args = [x, mod, g, w_in, qn, w_uq, kvn, w_ukv]
    if use_rope:
        in_specs += [pl.BlockSpec((tm, 128), lambda i, j: (j, 0))] * 2
        args += list(rope_tabs)
    hq = MLA_HEADS
    out_shape = (
        jax.ShapeDtypeStruct((b, hq, t, MLA_QK_PAD), BF16),
        jax.ShapeDtypeStruct((b, hq, t, MLA_QK_PAD), BF16),
        jax.ShapeDtypeStruct((b, hq, t, MLA_V), BF16),
        jax.ShapeDtypeStruct((b, t, SWA_HEADS * SWA_DH), BF16),
        jax.ShapeDtypeStruct((b, t, SWA_KV_HEADS * SWA_DH), BF16),
        jax.ShapeDtypeStruct((b, t, SWA_KV_HEADS * SWA_DH), BF16),
        jax.ShapeDtypeStruct((b, t, CONV_CH), F32),
    )
    out_specs = (
        pl.BlockSpec((1, hq, tm, MLA_QK_PAD), lambda i, j: (i, 0, j, 0)),
        pl.BlockSpec((1, hq, tm, MLA_QK_PAD), lambda i, j: (i, 0, j, 0)),
        pl.BlockSpec((1, hq, tm, MLA_V), lambda i, j: (i, 0, j, 0)),
        pl.BlockSpec((1, tm, SWA_HEADS * SWA_DH), lambda i, j: (i, j, 0)),
        pl.BlockSpec((1, tm, SWA_KV_HEADS * SWA_DH), lambda i, j: (i, j, 0)),
        pl.BlockSpec((1, tm, SWA_KV_HEADS * SWA_DH), lambda i, j: (i, j, 0)),
        pl.BlockSpec((1, tm, CONV_CH), lambda i, j: (i, j, 0)),
    )
    return pl.pallas_call(
        functools.partial(_premix_kernel, use_rope=use_rope),
        grid=(b, t // tm),
        in_specs=in_specs,
        out_specs=out_specs,
        out_shape=out_shape,
        compiler_params=_cparams(("arbitrary", "arbitrary")),
        name="premix_rope" if use_rope else "premix",
    )(*args)


def _mla_kernel(*refs, src_lens, tk):
    q_ref = refs[0]
    o_ref = refs[1 + 2 * len(src_lens)]
    q = q_ref[0, 0]
    m = l = acc = None
    for si, n_rows in enumerate(src_lens):
        k_ref, v_ref = refs[1 + 2 * si], refs[2 + 2 * si]
        for c0 in range(0, n_rows, tk):
            c1 = min(c0 + tk, n_rows)
            s = _dot_nt(q, k_ref[0, 0, c0:c1, :])
            v = v_ref[0, 0, c0:c1, :]
            mc = jnp.max(s, axis=1, keepdims=True)
            if m is None:
                m_new = mc
                p = jnp.exp(s - m_new)
                l = jnp.sum(p, axis=1, keepdims=True)
                acc = _dot(p.astype(BF16), v)
            else:
                m_new = jnp.maximum(m, mc)
                alpha = jnp.exp(m - m_new)
                p = jnp.exp(s - m_new)
                l = alpha * l + jnp.sum(p, axis=1, keepdims=True)
                acc = alpha * acc + _dot(p.astype(BF16), v)
            m = m_new
    o_ref[0] = acc / l


def _mla(q, kv_sources):
    b, hq, tq_all, dq = q.shape
    tq = MLA_Q_TILE
    in_specs = [pl.BlockSpec((1, 1, tq, dq), lambda i, h, j: (i, h, j, 0))]
    args = [q]
    src_lens = []
    for k, v in kv_sources:
        n = k.shape[2]
        src_lens.append(n)
        in_specs.append(pl.BlockSpec((1, 1, n, dq), lambda i, h, j: (i, h, 0, 0)))
        in_specs.append(pl.BlockSpec((1, 1, n, MLA_V), lambda i, h, j: (i, h, 0, 0)))
        args += [k, v]
    return pl.pallas_call(
        functools.partial(_mla_kernel, src_lens=tuple(src_lens), tk=MLA_K_TILE),
        grid=(b, hq, tq_all // tq),
        in_specs=in_specs,
        out_specs=pl.BlockSpec((1, tq, MLA_V), lambda i, h, j: (i, j, h)),
        out_shape=jax.ShapeDtypeStruct((b, tq_all, hq * MLA_V), F32),
        compiler_params=_cparams(("arbitrary", "arbitrary", "arbitrary")),
        name="mla_%d" % len(kv_sources),
    )(*args)


def _swa_kernel(*refs, s_len):
    latent = s_len > 0
    if latent:
        q_ref, kl_ref, vl_ref, kc_ref, vc_ref, sink_ref, o_ref = refs
        n = pl.program_id(1)
        win = 3 * SWA_BLOCK
        start = jnp.clip((n - 1) * SWA_BLOCK, 0, s_len - win)
        start = pl.multiple_of(start, SWA_BLOCK)
        kw = kl_ref[0, pl.ds(start, win), :]
        vw = vl_ref[0, pl.ds(start, win), :]
        rows = SWA_GROUP * SWA_BLOCK
        qpos = n * SWA_BLOCK + (lax.broadcasted_iota(jnp.int32, (rows, win), 0) & (SWA_BLOCK - 1))
        kpos = start + lax.broadcasted_iota(jnp.int32, (rows, win), 1)
        valid = jnp.abs(kpos - qpos) <= SWA_WINDOW
    else:
        q_ref, kc_ref, vc_ref, sink_ref, o_ref = refs
    kc = kc_ref[0]
    vc = vc_ref[0]
    q = q_ref[0]
    for kh in range(SWA_KV_HEADS):
        heads = range(kh * SWA_GROUP, (kh + 1) * SWA_GROUP)
        lo, hi = kh * SWA_DH, (kh + 1) * SWA_DH
        qs = jnp.concatenate([q[:, h * SWA_DH:(h + 1) * SWA_DH] for h in heads], axis=0)
        snk = jnp.concatenate(
            [jnp.broadcast_to(sink_ref[h:h + 1, 0:1], (SWA_BLOCK, 1)) for h in heads], axis=0)
        s_ctx = _dot_nt(qs, kc[:, lo:hi])
        mx = jnp.maximum(jnp.max(s_ctx, axis=1, keepdims=True), snk)
        if latent:
            s_loc = jnp.where(valid, _dot_nt(qs, kw[:, lo:hi]), NEG_INF)
            mx = jnp.maximum(mx, jnp.max(s_loc, axis=1, keepdims=True))
        p_ctx = jnp.exp(s_ctx - mx)
        den = jnp.sum(p_ctx, axis=1, keepdims=True) + jnp.exp(snk - mx)
        o = _dot(p_ctx.astype(BF16), vc[:, lo:hi])
        if latent:
            p_loc = jnp.exp(s_loc - mx)
            den = den + jnp.sum(p_loc, axis=1, keepdims=True)
            o = o + _dot(p_loc.astype(BF16), vw[:, lo:hi])
        o = o / den
        for g, h in enumerate(heads):
            o_ref[0, :, h * SWA_DH:(h + 1) * SWA_DH] = o[g * SWA_BLOCK:(g + 1) * SWA_BLOCK]


def _swa(q, k_lat, v_lat, k_ctx, v_ctx, sink_b):
    b, tq_all, dq = q.shape
    dkv = k_ctx.shape[2]
    n_ctx = k_ctx.shape[1]
    latent = k_lat is not None
    in_specs = [pl.BlockSpec((1, SWA_BLOCK, dq), lambda i, j: (i, j, 0))]
    args = [q]
    if latent:
        s_len = k_lat.shape[1]
        in_specs += [pl.BlockSpec((1, s_len, dkv), lambda i, j: (i, 0, 0))] * 2
        args += [k_lat, v_lat]
    else:
        s_len = 0
    in_specs += [pl.BlockSpec((1, n_ctx, dkv), lambda i, j: (i, 0, 0))] * 2
    in_specs += [pl.BlockSpec(sink_b.shape, lambda i, j: (0, 0))]
    args += [k_ctx, v_ctx, sink_b]
    return pl.pallas_call(
        functools.partial(_swa_kernel, s_len=s_len),
        grid=(b, tq_all // SWA_BLOCK),
        in_specs=in_specs,
        out_specs=pl.BlockSpec((1, SWA_BLOCK, dq), lambda i, j: (i, j, 0)),
        out_shape=jax.ShapeDtypeStruct((b, tq_all, dq), F32),
        compiler_params=_cparams(("arbitrary", "arbitrary")),
        name="swa_lat" if latent else "swa_ctx",
    )(*args)


CONV_ROWS = 32


def _conv_kernel(prev_ref, cur_ref, next_ref, w_ref, b_ref, lg_ref, lb_ref, o_ref, ext_ref, *, nt):
    j = pl.program_id(1)
    tm = cur_ref.shape[1]
    hl = CONV_HALO
    ext_ref[0:hl] = jnp.where(j > 0, prev_ref[0, tm - hl:tm, :], 0.0)
    ext_ref[hl:hl + tm] = cur_ref[0]
    ext_ref[hl + tm:2 * hl + tm] = jnp.where(j < nt - 1, next_ref[0, 0:hl, :], 0.0)
    off = hl - CONV_K // 2
    for r0 in range(0, tm, CONV_ROWS):
        acc = ext_ref[r0 + off:r0 + off + CONV_ROWS, :] * w_ref[0:1, :]
        for k in range(1, CONV_K):
            acc = acc + ext_ref[r0 + off + k:r0 + off + k + CONV_ROWS, :] * w_ref[k:k + 1, :]
        hcv = acc + b_ref[...]
        mu = jnp.mean(hcv, axis=-1, keepdims=True)
        xc = hcv - mu
        y = xc * lax.rsqrt(jnp.mean(xc * xc, axis=-1, keepdims=True) + EPS) * lg_ref[...] + lb_ref[...]
        o_ref[0, r0:r0 + CONV_ROWS, :] = y * jax.nn.sigmoid(y)


def _conv(glu, w, bias, ln_g, ln_b):
    b, t, ch = glu.shape
    tm = ROW_TILE
    nt = t // tm
    return pl.pallas_call(
        functools.partial(_conv_kernel, nt=nt),
        grid=(b, nt),
        in_specs=[
            pl.BlockSpec((1, tm, ch), lambda i, j: (i, jnp.maximum(j - 1, 0), 0)),
            pl.BlockSpec((1, tm, ch), lambda i, j: (i, j, 0)),
            pl.BlockSpec((1, tm, ch), lambda i, j: (i, jnp.minimum(j + 1, nt - 1), 0)),
            pl.BlockSpec(w.shape, lambda i, j: (0, 0)),
            pl.BlockSpec(bias.shape, lambda i, j: (0, 0)),
            pl.BlockSpec(ln_g.shape, lambda i, j: (0, 0)),
            pl.BlockSpec(ln_b.shape, lambda i, j: (0, 0)),
        ],
        out_specs=pl.BlockSpec((1, tm, ch), lambda i, j: (i, j, 0)),
        out_shape=jax.ShapeDtypeStruct((b, t, ch), F32),
        scratch_shapes=[pltpu.VMEM((tm + 2 * CONV_HALO, ch), F32)],
        compiler_params=_cparams(("arbitrary", "arbitrary")),
        name="conv",
    )(glu, glu, glu, w, bias, ln_g, ln_b)


def _merge_kernel(oa_ref, ob_ref, oc_ref, on_ref, wout_ref, x_ref, mod_ref, gm_ref, x1_ref, h2_ref):
    on = on_ref[...]
    na = oa_ref.shape[2]
    nb = ob_ref.shape[2]
    y = jnp.concatenate([
        _rms(oa_ref[0], on[:, 0:na]).astype(BF16),
        _rms(ob_ref[0], on[:, na:na + nb]).astype(BF16),
        _rms(oc_ref[0], on[:, na + nb:]).astype(BF16),
    ], axis=1)
    m = mod_ref[0]
    x1 = x_ref[0] + m[2:3] * _dot(y, wout_ref[...])
    x1_ref[0] = x1
    h2_ref[0] = (_rms(x1, gm_ref[...]) * (1.0 + m[4:5]) + m[3:4]).astype(BF16)


def _merge(oa, ob, oc, out_norm, w_out, x, mod, g_mlp):
    b, t, d = x.shape
    tm = ROW_TILE
    row = lambda i, j: (i, j, 0)
    return pl.pallas_call(
        _merge_kernel,
        grid=(b, t // tm),
        in_specs=[
            pl.BlockSpec((1, tm, oa.shape[2]), row),
            pl.BlockSpec((1, tm, ob.shape[2]), row),
            pl.BlockSpec((1, tm, oc.shape[2]), row),
            _const_spec(out_norm.shape),
            _const_spec(w_out.shape),
            pl.BlockSpec((1, tm, d), row),
            pl.BlockSpec((1, 6, d), lambda i, j: (i, 0, 0)),
            _const_spec(g_mlp.shape),
        ],
        out_specs=(pl.BlockSpec((1, tm, d), row), pl.BlockSpec((1, tm, d), row)),
        out_shape=(jax.ShapeDtypeStruct((b, t, d), F32), jax.ShapeDtypeStruct((b, t, d), BF16)),
        compiler_params=_cparams(("arbitrary", "arbitrary")),
        name="merge",
    )(oa, ob, oc, out_norm, w_out, x, mod, g_mlp)


def _mlp_kernel(*refs, nf, final):
    if final:
        h_ref, w1_ref, w2_ref, x_ref, mod_ref, fn_ref, o_ref, acc_ref = refs
    else:
        h_ref, w1_ref, w2_ref, x_ref, mod_ref, o_ref, acc_ref = refs
    j = pl.program_id(2)
    a = jnp.square(jnp.maximum(_dot(h_ref[0], w1_ref[...]), 0.0))
    part = _dot(a.astype(BF16), w2_ref[...])

    @pl.when(j == 0)
    def _():
        acc_ref[...] = part

    @pl.when(j > 0)
    def _():
        acc_ref[...] += part

    @pl.when(j == nf - 1)
    def _():
        out = x_ref[0] + mod_ref[0][5:6] * acc_ref[...]
        if final:
            out = _rms(out, fn_ref[...])
        o_ref[0] = out


def _mlp(h2, w1, w2, x1, mod, final_norm=None):
    b, t, d = x1.shape
    dff = w1.shape[1]
    tr = min(MLP_ROWS, t)
    tf = MLP_FF_TILE
    nf = dff // tf
    final = final_norm is not None
    row = lambda i, r, j: (i, r, 0)
    in_specs = [
        pl.BlockSpec((1, tr, d), row),
        pl.BlockSpec((d, tf), lambda i, r, j: (0, j)),
        pl.BlockSpec((tf, d), lambda i, r, j: (j, 0)),
        pl.BlockSpec((1, tr, d), row),
        pl.BlockSpec((1, 6, d), lambda i, r, j: (i, 0, 0)),
    ]
    args = [h2, w1, w2, x1, mod]
    if final:
        in_specs.append(pl.BlockSpec(final_norm.shape, lambda i, r, j: (0, 0)))
        args.append(final_norm)
    return pl.pallas_call(
        functools.partial(_mlp_kernel, nf=nf, final=final),
        grid=(b, t // tr, nf),
        in_specs=in_specs,
        out_specs=pl.BlockSpec((1, tr, d), row),
        out_shape=jax.ShapeDtypeStruct((b, t, d), F32),
        scratch_shapes=[pltpu.VMEM((tr, d), F32)],
        compiler_params=_cparams(("arbitrary", "arbitrary", "arbitrary")),
        name="mlp_final" if final else "mlp",
    )(*args)


def _rope_tables(n_tok):
    rows = n_tok // GRID_W
    row = jnp.repeat(jnp.arange(rows, dtype=F32), GRID_W)
    col = jnp.tile(jnp.arange(GRID_W, dtype=F32), rows)
    n_freq = MLA_ROPE // 4
    inv_freq = ROPE_THETA ** (-jnp.arange(n_freq, dtype=F32) / n_freq)
    ar = row[:, None] * inv_freq
    ac = col[:, None] * inv_freq
    cos = jnp.concatenate([jnp.cos(ar), jnp.cos(ar), jnp.cos(ac), jnp.cos(ac)], axis=1)
    sin = jnp.concatenate([-jnp.sin(ar), jnp.sin(ar), -jnp.sin(ac), jnp.sin(ac)], axis=1)
    return jnp.tile(cos, (1, 2)), jnp.tile(sin, (1, 2))


def kernel(x, c, ctx, c_ctx, ada_w, ada_b, norm_mix, norm_mlp, w_in, mla_q_norm, mla_w_uq, mla_kv_norm, mla_w_ukv, swa_sink, conv_w, conv_b, conv_ln_g, conv_ln_b, out_norm, w_out, mlp_w1, mlp_w2, final_norm):
    b, s, d = x.shape
    n_ctx = ctx.shape[1]
    depth = ada_w.shape[0]
    assert s % MLA_K_TILE == 0 and s % MLP_ROWS == 0 and n_ctx % ROW_TILE == 0
    assert s >= 3 * SWA_BLOCK and b + 1 <= 8

    cvec = jnp.concatenate([c, c_ctx[None, :], jnp.zeros((8 - b - 1, d), F32)], axis=0)
    mod = _ada(cvec, ada_w, ada_b)
    rope_tabs = _rope_tables(s)
    row_vec = lambda v: v.reshape(1, -1)

    xc = ctx
    for l in range(depth):
        update_ctx = l < depth - 1
        ml = mod[l].reshape(8, 6, d)
        mod_lat = ml[:b]
        mod_ctx = jnp.broadcast_to(ml[b:b + 1], (b, 6, d))

        wi = w_in[l]
        kr_end = OFF_KR + MLA_ROPE
        w_in_p = jnp.concatenate(
            [wi[:, :kr_end], jnp.zeros((d, OFF_BQ - kr_end), F32), wi[:, kr_end:]], axis=1).astype(BF16)
        w_uq_p = jnp.pad(mla_w_uq[l], ((0, 0), (0, 0), (0, MLA_QK_PAD - MLA_NOPE - MLA_ROPE)))
        w_uq_p = w_uq_p.reshape(MLA_RANK, MLA_HEADS * MLA_QK_PAD).astype(BF16)
        w_ukv_p = mla_w_ukv[l].reshape(MLA_RANK, MLA_HEADS * (MLA_NOPE + MLA_V)).astype(BF16)
        w_out_p = w_out[l].astype(BF16)
        w1_p = mlp_w1[l].astype(BF16)
        w2_p = mlp_w2[l].astype(BF16)
        sink_b = jnp.broadcast_to(swa_sink[l][:, None], (SWA_HEADS, 128))
        cw = conv_w[l].reshape(CONV_K, CONV_CH)

        pre = functools.partial(
            _premix, g=row_vec(norm_mix[l]), w_in=w_in_p, qn=row_vec(mla_q_norm[l]), w_uq=w_uq_p,
            kvn=row_vec(mla_kv_norm[l]), w_ukv=w_ukv_p)
        qa, ka, va, bq, bk, bv, glu = pre(x, mod_lat, rope_tabs=rope_tabs)
        qa_c, ka_c, va_c, bq_c, bk_c, bv_c, glu_c = pre(xc, mod_ctx, rope_tabs=None)

        conv = functools.partial(_conv, w=cw, bias=row_vec(conv_b[l]), ln_g=row_vec(conv_ln_g[l]),
                                 ln_b=row_vec(conv_ln_b[l]))
        merge = functools.partial(_merge, out_norm=row_vec(out_norm[l]), w_out=w_out_p,
                                  g_mlp=row_vec(norm_mlp[l]))

        oa = _mla(qa, [(ka, va), (ka_c, va_c)])
        ob = _swa(bq, bk, bv, bk_c, bv_c, sink_b)
        oc = conv(glu)
        x1, h2 = merge(oa, ob, oc, x=x, mod=mod_lat)
        x = _mlp(h2, w1_p, w2_p, x1, mod_lat, None if update_ctx else row_vec(final_norm))

        if update_ctx:
            oa_c = _mla(qa_c, [(ka_c, va_c)])
            ob_c = _swa(bq_c, None, None, bk_c, bv_c, sink_b)
            oc_c = conv(glu_c)
            xc1, h2c = merge(oa_c, ob_c, oc_c, x=xc, mod=mod_ctx)
            flat = lambda a: a.reshape(1, b * n_ctx, d)
            xc = _mlp(flat(h2c), w1_p, w2_p, flat(xc1), mod_ctx[:1]).reshape(b, n_ctx, d)
    return x
```

```python
import functools

import jax
import jax.numpy as jnp
from jax import lax
from jax.experimental import pallas as pl
from jax.experimental.pallas import tpu as pltpu

F32 = jnp.float32
BF16 = jnp.bfloat16

EPS = 1e-6
NEG_INF = -1e30
GRID_W = 64
ROPE_THETA = 10000.0

MLA_HEADS = 8
MLA_RANK = 512
MLA_NOPE = 128
MLA_ROPE = 64
MLA_V = 128
MLA_QK_PAD = 256
LOG2E = 1.4426950408889634
MLA_Q_SCALE = (MLA_NOPE + MLA_ROPE) ** -0.5 * LOG2E
SWA_HEADS = 8
SWA_KV_HEADS = 2
SWA_GROUP = SWA_HEADS // SWA_KV_HEADS
SWA_DH = 64
SWA_WINDOW = 128
SWA_BLOCK = 128
SWA_Q_SCALE = SWA_DH ** -0.5 * LOG2E
CONV_CH = 512
CONV_K = 31
CONV_HALO = 16

OFF_AQ = 0
OFF_AKV = 512
OFF_KR = 1024
OFF_BQ = 1152
OFF_BK = 1664
OFF_BV = 1792
OFF_C = 1920
IN_PAD_WIDTH = 2944

ROW_TILE = 256
MLP_ROWS = 512
MLP_FF_TILE = 1024
MLA_Q_HALF = 256
MLA_SM_ROWS = 16
SWA_STEP_BLOCKS = 4
ADA_N_TILE = 1024
VMEM_LIMIT = 56 * 1024 * 1024


def _cparams(sem):
    return pltpu.CompilerParams(dimension_semantics=sem, vmem_limit_bytes=VMEM_LIMIT)


def _const_spec(shape):
    nd = len(shape)
    return pl.BlockSpec(shape, lambda *_: (0,) * nd, pipeline_mode=pl.Buffered(1))


def _rms(x, g):
    return x * lax.rsqrt(jnp.mean(x * x, axis=-1, keepdims=True) + EPS) * g


def _dot(a, b):
    return jnp.dot(a, b, preferred_element_type=F32)


def _dot_nt(a, b):
    return lax.dot_general(a, b, (((1,), (1,)), ((), ())), preferred_element_type=F32)


def _ada_kernel(c_ref, w_ref, b_ref, o_ref):
    c = c_ref[...]
    s = c * jax.nn.sigmoid(c)
    o_ref[0] = _dot(s.astype(BF16), w_ref[0].astype(BF16)) + b_ref[0]


def _ada(cvec, ada_w, ada_b):
    n_layers, d, n = ada_w.shape
    rows = cvec.shape[0]
    tn = ADA_N_TILE
    return pl.pallas_call(
        _ada_kernel,
        grid=(n_layers, n // tn),
        in_specs=[
            pl.BlockSpec((rows, d), lambda l, j: (0, 0)),
            pl.BlockSpec((1, d, tn), lambda l, j: (l, 0, j)),
            pl.BlockSpec((1, 1, tn), lambda l, j: (l, 0, j)),
        ],
        out_specs=pl.BlockSpec((1, rows, tn), lambda l, j: (l, 0, j)),
        out_shape=jax.ShapeDtypeStruct((n_layers, rows, n), F32),
        compiler_params=_cparams(("arbitrary", "arbitrary")),
        name="ada",
    )(cvec, ada_w, ada_b.reshape(n_layers, 1, n))


def _rope128(t, cos, sin):
    lane = lax.broadcasted_iota(jnp.int32, t.shape, 1)
    up = pltpu.roll(t, 128 - 16, 1)
    dn = pltpu.roll(t, 16, 1)
    sw = jnp.where((lane & 16) == 0, up, dn)
    return t * cos + sw * sin


def _premix_kernel(*refs, use_rope):
    if use_rope:
        (x_ref, mod_ref, g_ref, win_ref, qn_ref, wuq_ref, kvn_ref, wukv_ref, cos_ref, sin_ref,
         qa_ref, ka_ref, va_ref, bq_ref, bk_ref, bv_ref, glu_ref) = refs
        cos = cos_ref[...]
        sin = sin_ref[...]
        rope = lambda t: _rope128(t, cos, sin)
    else:
        (x_ref, mod_ref, g_ref, win_ref, qn_ref, wuq_ref, kvn_ref, wukv_ref,
         qa_ref, ka_ref, va_ref, bq_ref, bk_ref, bv_ref, glu_ref) = refs
        rope = lambda t: t

    m = mod_ref[0]
    h = _rms(x_ref[0], g_ref[...]) * (1.0 + m[1:2]) + m[0:1]
    p = _dot(h.astype(BF16), win_ref[...])

    qn = _rms(p[:, OFF_AQ:OFF_AQ + MLA_RANK], qn_ref[...])
    q = _dot(qn.astype(BF16), wuq_ref[...])
    for hd in range(MLA_HEADS):
        c0 = hd * MLA_QK_PAD
        qa_ref[0, hd, :, 0:128] = (q[:, c0:c0 + 128] * MLA_Q_SCALE).astype(BF16)
        qa_ref[0, hd, :, 128:256] = (rope(q[:, c0 + 128:c0 + 256]) * MLA_Q_SCALE).astype(BF16)

    kvn = _rms(p[:, OFF_AKV:OFF_AKV + MLA_RANK], kvn_ref[...])
    kv = _dot(kvn.astype(BF16), wukv_ref[...])
    kpe = rope(p[:, OFF_KR:OFF_KR + 128]).astype(BF16)
    for hd in range(MLA_HEADS):
        c0 = hd * (MLA_NOPE + MLA_V)
        ka_ref[0, hd, :, 0:128] = kv[:, c0:c0 + 128].astype(BF16)
        ka_ref[0, hd, :, 128:256] = kpe
        va_ref[0, hd] = kv[:, c0 + 128:c0 + 256].astype(BF16)

    for t in range(SWA_HEADS * SWA_DH // 128):
        c0 = OFF_BQ + t * 128
        bq_ref[0, :, t * 128:(t + 1) * 128] = (rope(p[:, c0:c0 + 128]) * SWA_Q_SCALE).astype(BF16)
    bk_ref[0] = rope(p[:, OFF_BK:OFF_BK + 128]).astype(BF16)
    bv_ref[0] = p[:, OFF_BV:OFF_BV + 128].astype(BF16)

    glu_ref[0] = p[:, OFF_C:OFF_C + CONV_CH] * jax.nn.sigmoid(p[:, OFF_C + CONV_CH:OFF_C + 2 * CONV_CH])


def _premix(x, mod, g, w_in, qn, w_uq, kvn, w_ukv, rope_tabs):
    b, t, d = x.shape
    tm = ROW_TILE
    use_rope = rope_tabs is not None
    in_specs = [
        pl.BlockSpec((1, tm, d), lambda i, j: (i, j, 0)),
        pl.BlockSpec((1, 6, d), lambda i, j: (i, 0, 0)),
        _const_spec(g.shape),
        _const_spec(w_in.shape),
        _const_spec(qn.shape),
        _const_spec(w_uq.shape),
        _const_spec(kvn.shape),
        _const_spec(w_ukv.shape),
    ]
    args = [x, mod, g, w_in, qn, w_uq, kvn, w_ukv]
    if use_rope:
        in_specs += [pl.BlockSpec((tm, 128), lambda i, j: (j, 0))] * 2
        args += list(rope_tabs)
    hq = MLA_HEADS
    out_shape = (
        jax.ShapeDtypeStruct((b, hq, t, MLA_QK_PAD), BF16),
        jax.ShapeDtypeStruct((b, hq, t, MLA_QK_PAD), BF16),
        jax.ShapeDtypeStruct((b, hq, t, MLA_V), BF16),
        jax.ShapeDtypeStruct((b, t, SWA_HEADS * SWA_DH), BF16),
        jax.ShapeDtypeStruct((b, t, SWA_KV_HEADS * SWA_DH), BF16),
        jax.ShapeDtypeStruct((b, t, SWA_KV_HEADS * SWA_DH), BF16),
        jax.ShapeDtypeStruct((b, t, CONV_CH), F32),
    )
    out_specs = (
        pl.BlockSpec((1, hq, tm, MLA_QK_PAD), lambda i, j: (i, 0, j, 0)),
        pl.BlockSpec((1, hq, tm, MLA_QK_PAD), lambda i, j: (i, 0, j, 0)),
        pl.BlockSpec((1, hq, tm, MLA_V), lambda i, j: (i, 0, j, 0)),
        pl.BlockSpec((1, tm, SWA_HEADS * SWA_DH), lambda i, j: (i, j, 0)),
        pl.BlockSpec((1, tm, SWA_KV_HEADS * SWA_DH), lambda i, j: (i, j, 0)),
        pl.BlockSpec((1, tm, SWA_KV_HEADS * SWA_DH), lambda i, j: (i, j, 0)),
        pl.BlockSpec((1, tm, CONV_CH), lambda i, j: (i, j, 0)),
    )
    return pl.pallas_call(
        functools.partial(_premix_kernel, use_rope=use_rope),
        grid=(b, t // tm),
        in_specs=in_specs,
        out_specs=out_specs,
        out_shape=out_shape,
        compiler_params=_cparams(("arbitrary", "arbitrary")),
        name="premix_rope" if use_rope else "premix",
    )(*args)


def _mla_kernel(*refs, src_lens, half):
    n_src = len(src_lens)
    q_ref = refs[0]
    kv_refs = refs[1:1 + 2 * n_src]
    o_ref = refs[1 + 2 * n_src]
    s_refs = refs[2 + 2 * n_src:4 + 2 * n_src]
    p_refs = refs[4 + 2 * n_src:6 + 2 * n_src]
    l_refs = refs[6 + 2 * n_src:8 + 2 * n_src]
    offs = [sum(src_lens[:i]) for i in range(n_src)]

    def scores(hf):
        qh = q_ref[0, 0, hf * half:(hf + 1) * half, :]
        for si in range(n_src):
            s_refs[hf][:, offs[si]:offs[si] + src_lens[si]] = _dot_nt(qh, kv_refs[2 * si][0, 0])

    def softmax(hf):
        for r in range(0, half, MLA_SM_ROWS):
            sb = s_refs[hf][r:r + MLA_SM_ROWS, :]
            p = jnp.exp2(sb - jnp.max(sb, axis=1, keepdims=True))
            l_refs[hf][r:r + MLA_SM_ROWS, :] = jnp.broadcast_to(
                jnp.sum(p, axis=1, keepdims=True), (MLA_SM_ROWS, 128))
            p_refs[hf][r:r + MLA_SM_ROWS, :] = p.astype(BF16)

    def values(hf):
        acc = None
        for si in range(n_src):
            part = _dot(p_refs[hf][:, offs[si]:offs[si] + src_lens[si]], kv_refs[2 * si + 1][0, 0])
            acc = part if acc is None else acc + part
        o_ref[0, hf * half:(hf + 1) * half, :] = acc / l_refs[hf][...]

    scores(0)
    scores(1)
    softmax(0)
    values(0)
    softmax(1)
    values(1)


def _mla(q, kv_sources):
    b, hq, tq_all, dq = q.shape
    half = min(MLA_Q_HALF, tq_all // 2)
    tq = 2 * half
    in_specs = [pl.BlockSpec((1, 1, tq, dq), lambda i, h, j: (i, h, j, 0))]
    args = [q]
    src_lens = []
    for k, v in kv_sources:
        n = k.shape[2]
        src_lens.append(n)
        in_specs.append(pl.BlockSpec((1, 1, n, dq), lambda i, h, j: (i, h, 0, 0)))
        in_specs.append(pl.BlockSpec((1, 1, n, MLA_V), lambda i, h, j: (i, h, 0, 0)))
        args += [k, v]
    n_keys = sum(src_lens)
    return pl.pallas_call(
        functools.partial(_mla_kernel, src_lens=tuple(src_lens), half=half),
        grid=(b, hq, tq_all // tq),
        in_specs=in_specs,
        out_specs=pl.BlockSpec((1, tq, MLA_V), lambda i, h, j: (i, j, h)),
        out_shape=jax.ShapeDtypeStruct((b, tq_all, hq * MLA_V), F32),
        scratch_shapes=(
            [pltpu.VMEM((half, n_keys), F32)] * 2
            + [pltpu.VMEM((half, n_keys), BF16)] * 2
            + [pltpu.VMEM((half, 128), F32)] * 2),
        compiler_params=_cparams(("arbitrary", "arbitrary", "arbitrary")),
        name="mla_%d" % len(kv_sources),
    )(*args)


def _swa_kernel(*refs, s_len, nblk):
    latent = s_len > 0
    if latent:
        q_ref, kl_ref, vl_ref, kc_ref, vc_ref, sink_ref, o_ref = refs
    else:
        q_ref, kc_ref, vc_ref, sink_ref, o_ref = refs
    kc = kc_ref[0]
    vc = vc_ref[0]
    rows = SWA_GROUP * SWA_BLOCK
    win = 3 * SWA_BLOCK
    chains = [(blk, kh) for blk in range(nblk) for kh in range(SWA_KV_HEADS)]
    kw, vw, valid = {}, {}, {}
    if latent:
        d = (lax.broadcasted_iota(jnp.int32, (rows, win), 1)
             - (lax.broadcasted_iota(jnp.int32, (rows, win), 0) & (SWA_BLOCK - 1)))
        for blk in range(nblk):
            n = pl.program_id(1) * nblk + blk
            start = jnp.clip((n - 1) * SWA_BLOCK, 0, s_len - win)
            start = pl.multiple_of(start, SWA_BLOCK)
            kw[blk] = kl_ref[0, pl.ds(start, win), :]
            vw[blk] = vl_ref[0, pl.ds(start, win), :]
            valid[blk] = jnp.abs(d + (start - n * SWA_BLOCK)) <= SWA_WINDOW

    s_ctx, s_loc, snk, mx = {}, {}, {}, {}
    for c in chains:
        blk, kh = c
        heads = range(kh * SWA_GROUP, (kh + 1) * SWA_GROUP)
        lo, hi = kh * SWA_DH, (kh + 1) * SWA_DH
        q = q_ref[0, blk * SWA_BLOCK:(blk + 1) * SWA_BLOCK, :]
        qs = jnp.concatenate([q[:, h * SWA_DH:(h + 1) * SWA_DH] for h in heads], axis=0)
        snk[c] = jnp.concatenate(
            [jnp.broadcast_to(sink_ref[h:h + 1, 0:1] * LOG2E, (SWA_BLOCK, 1)) for h in heads], axis=0)
        s_ctx[c] = _dot_nt(qs, kc[:, lo:hi])
        if latent:
            s_loc[c] = jnp.where(valid[blk], _dot_nt(qs, kw[blk][:, lo:hi]), NEG_INF)
    for c in chains:
        m = jnp.maximum(jnp.max(s_ctx[c], axis=1, keepdims=True), snk[c])
        if latent:
            m = jnp.maximum(m, jnp.max(s_loc[c], axis=1, keepdims=True))
        mx[c] = m
    p_ctx, p_loc, den = {}, {}, {}
    for c in chains:
        p = jnp.exp2(s_ctx[c] - mx[c])
        dsum = jnp.sum(p, axis=1, keepdims=True) + jnp.exp2(snk[c] - mx[c])
        p_ctx[c] = p.astype(BF16)
        if latent:
            p = jnp.exp2(s_loc[c] - mx[c])
            dsum = dsum + jnp.sum(p, axis=1, keepdims=True)
            p_loc[c] = p.astype(BF16)
        den[c] = dsum
    for c in chains:
        blk, kh = c
        lo, hi = kh * SWA_DH, (kh + 1) * SWA_DH
        o = _dot(p_ctx[c], vc[:, lo:hi])
        if latent:
            o = o + _dot(p_loc[c], vw[blk][:, lo:hi])
        o = o / den[c]
        for g in range(SWA_GROUP):
            h = kh * SWA_GROUP + g
            o_ref[0, blk * SWA_BLOCK:(blk + 1) * SWA_BLOCK, h * SWA_DH:(h + 1) * SWA_DH] = (
                o[g * SWA_BLOCK:(g + 1) * SWA_BLOCK])


def _swa(q, k_lat, v_lat, k_ctx, v_ctx, sink_b):
    b, tq_all, dq = q.shape
    dkv = k_ctx.shape[2]
    n_ctx = k_ctx.shape[1]
    latent = k_lat is not None
    nblk = min(SWA_STEP_BLOCKS, tq_all // SWA_BLOCK)
    tq = nblk * SWA_BLOCK
    in_specs = [pl.BlockSpec((1, tq, dq), lambda i, j: (i, j, 0))]
    args = [q]
    if latent:
        s_len = k_lat.shape[1]
        in_specs += [pl.BlockSpec((1, s_len, dkv), lambda i, j: (i, 0, 0))] * 2
        args += [k_lat, v_lat]
    else:
        s_len = 0
    in_specs += [pl.BlockSpec((1, n_ctx, dkv), lambda i, j: (i, 0, 0))] * 2
    in_specs += [pl.BlockSpec(sink_b.shape, lambda i, j: (0, 0))]
    args += [k_ctx, v_ctx, sink_b]
    return pl.pallas_call(
        functools.partial(_swa_kernel, s_len=s_len, nblk=nblk),
        grid=(b, tq_all // tq),
        in_specs=in_specs,
        out_specs=pl.BlockSpec((1, tq, dq), lambda i, j: (i, j, 0)),
        out_shape=jax.ShapeDtypeStruct((b, tq_all, dq), F32),
        compiler_params=_cparams(("arbitrary", "arbitrary")),
        name="swa_lat" if latent else "swa_ctx",
    )(*args)


CONV_ROWS = 32


def _conv_kernel(prev_ref, cur_ref, next_ref, w_ref, b_ref, lg_ref, lb_ref, o_ref, ext_ref, sh_ref, *, nt):
    j = pl.program_id(1)
    tm = cur_ref.shape[1]
    hl = CONV_HALO
    ext_ref[0:hl] = jnp.where(j > 0, prev_ref[0, tm - hl:tm, :], 0.0)
    ext_ref[hl:hl + tm] = cur_ref[0]
    ext_ref[hl + tm:2 * hl + tm] = jnp.where(j < nt - 1, next_ref[0, 0:hl, :], 0.0)
    n_sh = sh_ref.shape[1]
    for sb in range(8):
        sh_ref[sb] = ext_ref[sb:sb + n_sh, :]
    off = hl - CONV_K // 2
    for r0 in range(0, tm, CONV_ROWS):
        acc = None
        for k in range(CONV_K):
            sb, a8 = (off + k) % 8, (off + k) // 8 * 8
            term = sh_ref[sb, r0 + a8:r0 + a8 + CONV_ROWS, :] * w_ref[k:k + 1, :]
            acc = term if acc is None else acc + term
        hcv = acc + b_ref[...]
        mu = jnp.mean(hcv, axis=-1, keepdims=True)
        xc = hcv - mu
        y = xc * lax.rsqrt(jnp.mean(xc * xc, axis=-1, keepdims=True) + EPS) * lg_ref[...] + lb_ref[...]
        o_ref[0, r0:r0 + CONV_ROWS, :] = y * jax.nn.sigmoid(y)


def _conv(glu, w, bias, ln_g, ln_b):
    b, t, ch = glu.shape
    tm = ROW_TILE
    nt = t // tm
    return pl.pallas_call(
        functools.partial(_conv_kernel, nt=nt),
        grid=(b, nt),
        in_specs=[
            pl.BlockSpec((1, tm, ch), lambda i, j: (i, jnp.maximum(j - 1, 0), 0)),
            pl.BlockSpec((1, tm, ch), lambda i, j: (i, j, 0)),
            pl.BlockSpec((1, tm, ch), lambda i, j: (i, jnp.minimum(j + 1, nt - 1), 0)),
            pl.BlockSpec(w.shape, lambda i, j: (0, 0)),
            pl.BlockSpec(bias.shape, lambda i, j: (0, 0)),
            pl.BlockSpec(ln_g.shape, lambda i, j: (0, 0)),
            pl.BlockSpec(ln_b.shape, lambda i, j: (0, 0)),
        ],
        out_specs=pl.BlockSpec((1, tm, ch), lambda i, j: (i, j, 0)),
        out_shape=jax.ShapeDtypeStruct((b, t, ch), F32),
        scratch_shapes=[pltpu.VMEM((tm + 2 * CONV_HALO, ch), F32),
                        pltpu.VMEM((8, tm + 2 * CONV_HALO - 8, ch), F32)],
        compiler_params=_cparams(("arbitrary", "arbitrary")),
        name="conv",
    )(glu, glu, glu, w, bias, ln_g, ln_b)


def _merge_kernel(oa_ref, ob_ref, oc_ref, on_ref, wout_ref, x_ref, mod_ref, gm_ref, x1_ref, h2_ref):
    on = on_ref[...]
    na = oa_ref.shape[2]
    nb = ob_ref.shape[2]
    y = jnp.concatenate([
        _rms(oa_ref[0], on[:, 0:na]).astype(BF16),
        _rms(ob_ref[0], on[:, na:na + nb]).astype(BF16),
        _rms(oc_ref[0], on[:, na + nb:]).astype(BF16),
    ], axis=1)
    m = mod_ref[0]
    x1 = x_ref[0] + m[2:3] * _dot(y, wout_ref[...])
    x1_ref[0] = x1
    h2_ref[0] = (_rms(x1, gm_ref[...]) * (1.0 + m[4:5]) + m[3:4]).astype(BF16)


def _merge(oa, ob, oc, out_norm, w_out, x, mod, g_mlp):
    b, t, d = x.shape
    tm = ROW_TILE
    row = lambda i, j: (i, j, 0)
    return pl.pallas_call(
        _merge_kernel,
        grid=(b, t // tm),
        in_specs=[
            pl.BlockSpec((1, tm, oa.shape[2]), row),
            pl.BlockSpec((1, tm, ob.shape[2]), row),
            pl.BlockSpec((1, tm, oc.shape[2]), row),
            _const_spec(out_norm.shape),
            _const_spec(w_out.shape),
            pl.BlockSpec((1, tm, d), row),
            pl.BlockSpec((1, 6, d), lambda i, j: (i, 0, 0)),
            _const_spec(g_mlp.shape),
        ],
        out_specs=(pl.BlockSpec((1, tm, d), row), pl.BlockSpec((1, tm, d), row)),
        out_shape=(jax.ShapeDtypeStruct((b, t, d), F32), jax.ShapeDtypeStruct((b, t, d), BF16)),
        compiler_params=_cparams(("arbitrary", "arbitrary")),
        name="merge",
    )(oa, ob, oc, out_norm, w_out, x, mod, g_mlp)


def _mlp_kernel(*refs, nf, final):
    if final:
        h_ref, w1_ref, w2_ref, x_ref, mod_ref, fn_ref, o_ref, acc_ref = refs
    else:
        h_ref, w1_ref, w2_ref, x_ref, mod_ref, o_ref, acc_ref = refs
    j = pl.program_id(2)

    @pl.when(j == 0)
    def _():
        acc_ref[...] = jnp.zeros_like(acc_ref)

    a = jnp.square(jnp.maximum(_dot(h_ref[0], w1_ref[...]), 0.0))
    acc_ref[...] += _dot(a.astype(BF16), w2_ref[...])

    @pl.when(j == nf - 1)
    def _():
        out = x_ref[0] + mod_ref[0][5:6] * acc_ref[...]
        if final:
            out = _rms(out, fn_ref[...])
        o_ref[0] = out


def _mlp(h2, w1, w2, x1, mod, final_norm=None):
    b, t, d = x1.shape
    dff = w1.shape[1]
    tr = min(MLP_ROWS, t)
    tf = MLP_FF_TILE
    nf = dff // tf
    final = final_norm is not None
    row = lambda i, r, j: (i, r, 0)
    in_specs = [
        pl.BlockSpec((1, tr, d), row),
        pl.BlockSpec((d, tf), lambda i, r, j: (0, j)),
        pl.BlockSpec((tf, d), lambda i, r, j: (j, 0)),
        pl.BlockSpec((1, tr, d), row),
        pl.BlockSpec((1, 6, d), lambda i, r, j: (i, 0, 0)),
    ]
    args = [h2, w1, w2, x1, mod]
    if final:
        in_specs.append(pl.BlockSpec(final_norm.shape, lambda i, r, j: (0, 0)))
        args.append(final_norm)
    return pl.pallas_call(
        functools.partial(_mlp_kernel, nf=nf, final=final),
        grid=(b, t // tr, nf),
        in_specs=in_specs,
        out_specs=pl.BlockSpec((1, tr, d), row),
        out_shape=jax.ShapeDtypeStruct((b, t, d), F32),
        scratch_shapes=[pltpu.VMEM((tr, d), F32)],
        compiler_params=_cparams(("arbitrary", "arbitrary", "arbitrary")),
        name="mlp_final" if final else "mlp",
    )(*args)


def _rope_tables(n_tok):
    rows = n_tok // GRID_W
    row = jnp.repeat(jnp.arange(rows, dtype=F32), GRID_W)
    col = jnp.tile(jnp.arange(GRID_W, dtype=F32), rows)
    n_freq = MLA_ROPE // 4
    inv_freq = ROPE_THETA ** (-jnp.arange(n_freq, dtype=F32) / n_freq)
    ar = row[:, None] * inv_freq
    ac = col[:, None] * inv_freq
    cos = jnp.concatenate([jnp.cos(ar), jnp.cos(ar), jnp.cos(ac), jnp.cos(ac)], axis=1)
    sin = jnp.concatenate([-jnp.sin(ar), jnp.sin(ar), -jnp.sin(ac), jnp.sin(ac)], axis=1)
    return jnp.tile(cos, (1, 2)), jnp.tile(sin, (1, 2))


def kernel(x, c, ctx, c_ctx, ada_w, ada_b, norm_mix, norm_mlp, w_in, mla_q_norm, mla_w_uq, mla_kv_norm, mla_w_ukv, swa_sink, conv_w, conv_b, conv_ln_g, conv_ln_b, out_norm, w_out, mlp_w1, mlp_w2, final_norm):
    b, s, d = x.shape
    n_ctx = ctx.shape[1]
    depth = ada_w.shape[0]
    assert s % (2 * MLA_Q_HALF) == 0 and s % MLP_ROWS == 0 and n_ctx % ROW_TILE == 0
    assert s % (SWA_STEP_BLOCKS * SWA_BLOCK) == 0
    assert s >= 3 * SWA_BLOCK and b + 1 <= 8

    cvec = jnp.concatenate([c, c_ctx[None, :], jnp.zeros((8 - b - 1, d), F32)], axis=0)
    mod = _ada(cvec, ada_w, ada_b)
    rope_tabs = _rope_tables(s)
    row_vec = lambda v: v.reshape(1, -1)

    xc = ctx
    for l in range(depth):
        update_ctx = l < depth - 1
        ml = mod[l].reshape(8, 6, d)
        mod_lat = ml[:b]
        mod_ctx = jnp.broadcast_to(ml[b:b + 1], (b, 6, d))

        wi = w_in[l]
        kr_end = OFF_KR + MLA_ROPE
        w_in_p = jnp.concatenate(
            [wi[:, :kr_end], jnp.zeros((d, OFF_BQ - kr_end), F32), wi[:, kr_end:]], axis=1).astype(BF16)
        w_uq_p = jnp.pad(mla_w_uq[l], ((0, 0), (0, 0), (0, MLA_QK_PAD - MLA_NOPE - MLA_ROPE)))
        w_uq_p = w_uq_p.reshape(MLA_RANK, MLA_HEADS * MLA_QK_PAD).astype(BF16)
        w_ukv_p = mla_w_ukv[l].reshape(MLA_RANK, MLA_HEADS * (MLA_NOPE + MLA_V)).astype(BF16)
        w_out_p = w_out[l].astype(BF16)
        w1_p = mlp_w1[l].astype(BF16)
        w2_p = mlp_w2[l].astype(BF16)
        sink_b = jnp.broadcast_to(swa_sink[l][:, None], (SWA_HEADS, 128))
        cw = conv_w[l].reshape(CONV_K, CONV_CH)

        pre = functools.partial(
            _premix, g=row_vec(norm_mix[l]), w_in=w_in_p, qn=row_vec(mla_q_norm[l]), w_uq=w_uq_p,
            kvn=row_vec(mla_kv_norm[l]), w_ukv=w_ukv_p)
        qa, ka, va, bq, bk, bv, glu = pre(x, mod_lat, rope_tabs=rope_tabs)
        qa_c, ka_c, va_c, bq_c, bk_c, bv_c, glu_c = pre(xc, mod_ctx, rope_tabs=None)

        conv = functools.partial(_conv, w=cw, bias=row_vec(conv_b[l]), ln_g=row_vec(conv_ln_g[l]),
                                 ln_b=row_vec(conv_ln_b[l]))
        merge = functools.partial(_merge, out_norm=row_vec(out_norm[l]), w_out=w_out_p,
                                  g_mlp=row_vec(norm_mlp[l]))

        oa = _mla(qa, [(ka, va), (ka_c, va_c)])
        ob = _swa(bq, bk, bv, bk_c, bv_c, sink_b)
        oc = conv(glu)
        x1, h2 = merge(oa, ob, oc, x=x, mod=mod_lat)
        x = _mlp(h2, w1_p, w2_p, x1, mod_lat, None if update_ctx else row_vec(final_norm))

        if update_ctx:
            oa_c = _mla(qa_c, [(ka_c, va_c)])
            ob_c = _swa(bq_c, None, None, bk_c, bv_c, sink_b)
            oc_c = conv(glu_c)
            xc1, h2c = merge(oa_c, ob_c, oc_c, x=xc, mod=mod_ctx)
            flat = lambda a: a.reshape(1, b * n_ctx, d)
            xc = _mlp(flat(h2c), w1_p, w2_p, flat(xc1), mod_ctx[:1]).reshape(b, n_ctx, d)
    return x
```

```python
import functools

import jax
import jax.numpy as jnp
from jax import lax
from jax.experimental import pallas as pl
from jax.experimental.pallas import tpu as pltpu

F32 = jnp.float32
BF16 = jnp.bfloat16

EPS = 1e-6
NEG_INF = -1e30
GRID_W = 64
ROPE_THETA = 10000.0

MLA_HEADS = 8
MLA_RANK = 512
MLA_NOPE = 128
MLA_ROPE = 64
MLA_V = 128
MLA_QK_PAD = 256
LOG2E = 1.4426950408889634
MLA_Q_SCALE = (MLA_NOPE + MLA_ROPE) ** -0.5 * LOG2E
SWA_HEADS = 8
SWA_KV_HEADS = 2
SWA_GROUP = SWA_HEADS // SWA_KV_HEADS
SWA_DH = 64
SWA_WINDOW = 128
SWA_BLOCK = 128
SWA_Q_SCALE = SWA_DH ** -0.5 * LOG2E
CONV_CH = 512
CONV_K = 31
CONV_HALO = 16

OFF_AQ = 0
OFF_AKV = 512
OFF_KR = 1024
OFF_BQ = 1152
OFF_BK = 1664
OFF_BV = 1792
OFF_C = 1920
IN_PAD_WIDTH = 2944

ROW_TILE = 256
MLP_ROWS = 512
MLP_FF_TILE = 1024
MLA_Q_TILE = 256
MLA_K_CHUNK = 256
MLA_SM_ROWS = 16
SWA_STEP_BLOCKS = 4
ADA_N_TILE = 1024
VMEM_LIMIT = 56 * 1024 * 1024


def _cparams(sem):
    return pltpu.CompilerParams(dimension_semantics=sem, vmem_limit_bytes=VMEM_LIMIT)


def _layer_spec(stack, l):
    _, a, b = stack.shape
    return pl.BlockSpec((None, a, b), lambda *_: (l, 0, 0), pipeline_mode=pl.Buffered(1))


def _mod_spec(mod, l, row):
    d = mod.shape[-1]
    return pl.BlockSpec((None, None, 6, d), lambda i, *_: (l, i if row is None else row, 0, 0))


def _rms(x, g):
    return x * lax.rsqrt(jnp.mean(x * x, axis=-1, keepdims=True) + EPS) * g


def _dot(a, b):
    return jnp.dot(a, b, preferred_element_type=F32)


def _dot_nt(a, b):
    return lax.dot_general(a, b, (((1,), (1,)), ((), ())), preferred_element_type=F32)


def _ada_kernel(c_ref, w_ref, b_ref, o_ref):
    c = c_ref[...]
    s = c * jax.nn.sigmoid(c)
    o_ref[0] = _dot(s.astype(BF16), w_ref[0].astype(BF16)) + b_ref[0]


def _ada(cvec, ada_w, ada_b):
    n_layers, d, n = ada_w.shape
    rows = cvec.shape[0]
    tn = ADA_N_TILE
    return pl.pallas_call(
        _ada_kernel,
        grid=(n_layers, n // tn),
        in_specs=[
            pl.BlockSpec((rows, d), lambda l, j: (0, 0)),
            pl.BlockSpec((1, d, tn), lambda l, j: (l, 0, j)),
            pl.BlockSpec((1, 1, tn), lambda l, j: (l, 0, j)),
        ],
        out_specs=pl.BlockSpec((1, rows, tn), lambda l, j: (l, 0, j)),
        out_shape=jax.ShapeDtypeStruct((n_layers, rows, n), F32),
        compiler_params=_cparams(("arbitrary", "arbitrary")),
        name="ada",
    )(cvec, ada_w, ada_b.reshape(n_layers, 1, n))


def _rope128(t, cos, sin):
    lane = lax.broadcasted_iota(jnp.int32, t.shape, 1)
    up = pltpu.roll(t, 128 - 16, 1)
    dn = pltpu.roll(t, 16, 1)
    sw = jnp.where((lane & 16) == 0, up, dn)
    return t * cos + sw * sin


def _premix_kernel(*refs, use_rope):
    if use_rope:
        (x_ref, mod_ref, g_ref, win_ref, qn_ref, wuq_ref, kvn_ref, wukv_ref, cos_ref, sin_ref,
         qa_ref, ka_ref, va_ref, bq_ref, bk_ref, bv_ref, glu_ref) = refs
        cos = cos_ref[...]
        sin = sin_ref[...]
        rope = lambda t: _rope128(t, cos, sin)
    else:
        (x_ref, mod_ref, g_ref, win_ref, qn_ref, wuq_ref, kvn_ref, wukv_ref,
         qa_ref, ka_ref, va_ref, bq_ref, bk_ref, bv_ref, glu_ref) = refs
        rope = lambda t: t

    m = mod_ref[...]
    h = _rms(x_ref[0], g_ref[...]) * (1.0 + m[1:2]) + m[0:1]
    p = _dot(h.astype(BF16), win_ref[...])

    qn = _rms(p[:, OFF_AQ:OFF_AQ + MLA_RANK], qn_ref[...])
    q = _dot(qn.astype(BF16), wuq_ref[...])
    for hd in range(MLA_HEADS):
        c0 = hd * MLA_QK_PAD
        qa_ref[0, hd, :, 0:128] = (q[:, c0:c0 + 128] * MLA_Q_SCALE).astype(BF16)
        qa_ref[0, hd, :, 128:256] = (rope(q[:, c0 + 128:c0 + 256]) * MLA_Q_SCALE).astype(BF16)

    kvn = _rms(p[:, OFF_AKV:OFF_AKV + MLA_RANK], kvn_ref[...])
    kv = _dot(kvn.astype(BF16), wukv_ref[...])
    kpe = rope(p[:, OFF_KR:OFF_KR + 128]).astype(BF16)
    for hd in range(MLA_HEADS):
        c0 = hd * (MLA_NOPE + MLA_V)
        ka_ref[0, hd, :, 0:128] = kv[:, c0:c0 + 128].astype(BF16)
        ka_ref[0, hd, :, 128:256] = kpe
        va_ref[0, hd] = kv[:, c0 + 128:c0 + 256].astype(BF16)

    for t in range(SWA_HEADS * SWA_DH // 128):
        c0 = OFF_BQ + t * 128
        bq_ref[0, :, t * 128:(t + 1) * 128] = (rope(p[:, c0:c0 + 128]) * SWA_Q_SCALE).astype(BF16)
    bk_ref[0] = rope(p[:, OFF_BK:OFF_BK + 128]).astype(BF16)
    bv_ref[0] = p[:, OFF_BV:OFF_BV + 128].astype(BF16)

    glu_ref[0] = p[:, OFF_C:OFF_C + CONV_CH] * jax.nn.sigmoid(p[:, OFF_C + CONV_CH:OFF_C + 2 * CONV_CH])


def _premix(x, mod, mod_row, params, l, rope_tabs):
    b, t, d = x.shape
    tm = ROW_TILE
    use_rope = rope_tabs is not None
    stacks = [params[k] for k in ("norm_mix", "w_in", "q_norm", "w_uq", "kv_norm", "w_ukv")]
    in_specs = [pl.BlockSpec((1, tm, d), lambda i, j: (i, j, 0)), _mod_spec(mod, l, mod_row)]
    in_specs += [_layer_spec(a, l) for a in stacks]
    args = [x, mod] + stacks
    if use_rope:
        in_specs += [pl.BlockSpec((tm, 128), lambda i, j: (j, 0))] * 2
        args += list(rope_tabs)
    hq = MLA_HEADS
    out_shape = (
        jax.ShapeDtypeStruct((b, hq, t, MLA_QK_PAD), BF16),
        jax.ShapeDtypeStruct((b, hq, t, MLA_QK_PAD), BF16),
        jax.ShapeDtypeStruct((b, hq, t, MLA_V), BF16),
        jax.ShapeDtypeStruct((b, t, SWA_HEADS * SWA_DH), BF16),
        jax.ShapeDtypeStruct((b, t, SWA_KV_HEADS * SWA_DH), BF16),
        jax.ShapeDtypeStruct((b, t, SWA_KV_HEADS * SWA_DH), BF16),
        jax.ShapeDtypeStruct((b, t, CONV_CH), F32),
    )
    out_specs = (
        pl.BlockSpec((1, hq, tm, MLA_QK_PAD), lambda i, j: (i, 0, j, 0)),
        pl.BlockSpec((1, hq, tm, MLA_QK_PAD), lambda i, j: (i, 0, j, 0)),
        pl.BlockSpec((1, hq, tm, MLA_V), lambda i, j: (i, 0, j, 0)),
        pl.BlockSpec((1, tm, SWA_HEADS * SWA_DH), lambda i, j: (i, j, 0)),
        pl.BlockSpec((1, tm, SWA_KV_HEADS * SWA_DH), lambda i, j: (i, j, 0)),
        pl.BlockSpec((1, tm, SWA_KV_HEADS * SWA_DH), lambda i, j: (i, j, 0)),
        pl.BlockSpec((1, tm, CONV_CH), lambda i, j: (i, j, 0)),
    )
    return pl.pallas_call(
        functools.partial(_premix_kernel, use_rope=use_rope),
        grid=(b, t // tm),
        in_specs=in_specs,
        out_specs=out_specs,
        out_shape=out_shape,
        compiler_params=_cparams(("arbitrary", "arbitrary")),
        name="premix_rope" if use_rope else "premix",
    )(*args)


def _mla_kernel(*refs, src_lens, tile):
    n_src = len(src_lens)
    q_ref = refs[0]
    kv_refs = refs[1:1 + 2 * n_src]
    o_ref = refs[1 + 2 * n_src]
    s_refs = refs[2 + 2 * n_src:4 + 2 * n_src]
    p_refs = refs[4 + 2 * n_src:6 + 2 * n_src]
    l_refs = refs[6 + 2 * n_src:8 + 2 * n_src]
    offs = [sum(src_lens[:i]) for i in range(n_src)]
    assert q_ref.shape[2] == 2 * tile

    def rows(t):
        return slice(t * tile, (t + 1) * tile)

    def key_chunks():
        for si in range(n_src):
            for c in range(0, src_lens[si], MLA_K_CHUNK):
                c1 = min(c + MLA_K_CHUNK, src_lens[si])
                yield si, slice(c, c1), slice(offs[si] + c, offs[si] + c1)

    def scores(t, slot):
        qt = q_ref[0, 0, rows(t), :]
        for si, src, dst in key_chunks():
            s_refs[slot][:, dst] = _dot_nt(qt, kv_refs[2 * si][0, 0, src, :])
            yield

    def softmax(slot):
        n_keys = s_refs[slot].shape[1]
        cols = [slice(c, c + 128) for c in range(0, n_keys, 128)]
        for r in range(0, tile, MLA_SM_ROWS):
            rs = slice(r, r + MLA_SM_ROWS)
            mm = s_refs[slot][rs, cols[0]]
            for cs in cols[1:]:
                mm = jnp.maximum(mm, s_refs[slot][rs, cs])
            m = jnp.broadcast_to(jnp.max(mm, axis=1, keepdims=True), (MLA_SM_ROWS, 128))
            lsum = None
            for cs in cols:
                p = jnp.exp2(s_refs[slot][rs, cs] - m)
                lsum = p if lsum is None else lsum + p
                p_refs[slot][rs, cs] = p.astype(BF16)
            l_refs[slot][rs, :] = jnp.broadcast_to(jnp.sum(lsum, axis=1, keepdims=True), (MLA_SM_ROWS, 128))
            yield

    def values(t, slot):
        acc = None
        for si, src, dst in key_chunks():
            part = _dot(p_refs[slot][:, dst], kv_refs[2 * si + 1][0, 0, src, :])
            acc = part if acc is None else acc + part
            yield
        o_ref[0, rows(t), :] = acc / l_refs[slot][...]

    def emit(*stages):
        stages = list(stages)
        while stages:
            for g in list(stages):
                if next(g, StopIteration) is StopIteration:
                    stages.remove(g)

    emit(scores(0, 0))
    emit(scores(1, 1), softmax(0))
    emit(values(0, 0), softmax(1))
    emit(values(1, 1))


def _mla(q, kv_sources):
    b, hq, tq_all, dq = q.shape
    tile = min(MLA_Q_TILE, tq_all // 2)
    tq = 2 * tile
    assert tq_all % tq == 0
    in_specs = [pl.BlockSpec((1, 1, tq, dq), lambda i, h, j: (i, h, j, 0))]
    args = [q]
    src_lens = []
    for k, v in kv_sources:
        n = k.shape[2]
        src_lens.append(n)
        in_specs.append(pl.BlockSpec((1, 1, n, dq), lambda i, h, j: (i, h, 0, 0)))
        in_specs.append(pl.BlockSpec((1, 1, n, MLA_V), lambda i, h, j: (i, h, 0, 0)))
        args += [k, v]
    n_keys = sum(src_lens)
    return pl.pallas_call(
        functools.partial(_mla_kernel, src_lens=tuple(src_lens), tile=tile),
        grid=(b, hq, tq_all // tq),
        in_specs=in_specs,
        out_specs=pl.BlockSpec((1, tq, MLA_V), lambda i, h, j: (i, j, h)),
        out_shape=jax.ShapeDtypeStruct((b, tq_all, hq * MLA_V), F32),
        scratch_shapes=(
            [pltpu.VMEM((tile, n_keys), F32)] * 2
            + [pltpu.VMEM((tile, n_keys), BF16)] * 2
            + [pltpu.VMEM((tile, 128), F32)] * 2),
        compiler_params=_cparams(("arbitrary", "arbitrary", "arbitrary")),
        name="mla_%d" % len(kv_sources),
    )(*args)


def _swa_kernel(*refs, s_len, nblk):
    latent = s_len > 0
    if latent:
        q_ref, kl_ref, vl_ref, kc_ref, vc_ref, sink_ref, o_ref = refs
    else:
        q_ref, kc_ref, vc_ref, sink_ref, o_ref = refs
    kc = kc_ref[0]
    vc = vc_ref[0]
    rows = SWA_GROUP * SWA_BLOCK
    win = 3 * SWA_BLOCK
    chains = [(blk, kh) for blk in range(nblk) for kh in range(SWA_KV_HEADS)]
    kcat, vcat, valid = {}, {}, {}
    n_win_tiles = win // 128 if latent else 0
    if latent:
        d = (lax.broadcasted_iota(jnp.int32, (rows, win), 1)
             - (lax.broadcasted_iota(jnp.int32, (rows, win), 0) & (SWA_BLOCK - 1)))
        for blk in range(nblk):
            n = pl.program_id(1) * nblk + blk
            start = jnp.clip((n - 1) * SWA_BLOCK, 0, s_len - win)
            start = pl.multiple_of(start, SWA_BLOCK)
            kcat[blk] = jnp.concatenate([kl_ref[0, pl.ds(start, win), :], kc], axis=0)
            vcat[blk] = jnp.concatenate([vl_ref[0, pl.ds(start, win), :], vc], axis=0)
            valid[blk] = jnp.abs(d + (start - n * SWA_BLOCK)) <= SWA_WINDOW
    else:
        for blk in range(nblk):
            kcat[blk], vcat[blk] = kc, vc

    s, snk, mx = {}, {}, {}
    for c in chains:
        blk, kh = c
        heads = range(kh * SWA_GROUP, (kh + 1) * SWA_GROUP)
        lo, hi = kh * SWA_DH, (kh + 1) * SWA_DH
        q = q_ref[0, blk * SWA_BLOCK:(blk + 1) * SWA_BLOCK, :]
        qs = jnp.concatenate([q[:, h * SWA_DH:(h + 1) * SWA_DH] for h in heads], axis=0)
        snk[c] = jnp.concatenate(
            [jnp.broadcast_to(sink_ref[h:h + 1, 0:1] * LOG2E, (SWA_BLOCK, 1)) for h in heads], axis=0)
        sc = _dot_nt(qs, kcat[blk][:, lo:hi])
        tiles = [sc[:, j * 128:(j + 1) * 128] for j in range(sc.shape[1] // 128)]
        for j in range(n_win_tiles):
            tiles[j] = jnp.where(valid[blk][:, j * 128:(j + 1) * 128], tiles[j], NEG_INF)
        s[c] = tiles
    for c in chains:
        mm = functools.reduce(jnp.maximum, s[c])
        mx[c] = jnp.maximum(jnp.max(mm, axis=1, keepdims=True), snk[c])
    prob, den = {}, {}
    for c in chains:
        ps = [jnp.exp2(t - mx[c]) for t in s[c]]
        den[c] = jnp.sum(functools.reduce(jnp.add, ps), axis=1, keepdims=True) + jnp.exp2(snk[c] - mx[c])
        prob[c] = jnp.concatenate([p.astype(BF16) for p in ps], axis=1)
    for c in chains:
        blk, kh = c
        lo, hi = kh * SWA_DH, (kh + 1) * SWA_DH
        o = _dot(prob[c], vcat[blk][:, lo:hi]) / den[c]
        for g in range(SWA_GROUP):
            h = kh * SWA_GROUP + g
            o_ref[0, blk * SWA_BLOCK:(blk + 1) * SWA_BLOCK, h * SWA_DH:(h + 1) * SWA_DH] = (
                o[g * SWA_BLOCK:(g + 1) * SWA_BLOCK])


def _swa(q, k_lat, v_lat, k_ctx, v_ctx, params, l):
    sink_b = params["sink"]
    b, tq_all, dq = q.shape
    dkv = k_ctx.shape[2]
    n_ctx = k_ctx.shape[1]
    latent = k_lat is not None
    nblk = min(SWA_STEP_BLOCKS, tq_all // SWA_BLOCK)
    tq = nblk * SWA_BLOCK
    in_specs = [pl.BlockSpec((1, tq, dq), lambda i, j: (i, j, 0))]
    args = [q]
    if latent:
        s_len = k_lat.shape[1]
        in_specs += [pl.BlockSpec((1, s_len, dkv), lambda i, j: (i, 0, 0))] * 2
        args += [k_lat, v_lat]
    else:
        s_len = 0
    in_specs += [pl.BlockSpec((1, n_ctx, dkv), lambda i, j: (i, 0, 0))] * 2
    in_specs += [_layer_spec(sink_b, l)]
    args += [k_ctx, v_ctx, sink_b]
    return pl.pallas_call(
        functools.partial(_swa_kernel, s_len=s_len, nblk=nblk),
        grid=(b, tq_all // tq),
        in_specs=in_specs,
        out_specs=pl.BlockSpec((1, tq, dq), lambda i, j: (i, j, 0)),
        out_shape=jax.ShapeDtypeStruct((b, tq_all, dq), F32),
        compiler_params=_cparams(("arbitrary", "arbitrary")),
        name="swa_lat" if latent else "swa_ctx",
    )(*args)


CONV_ROWS = 32


def _conv_kernel(prev_ref, cur_ref, next_ref, w_ref, b_ref, lg_ref, lb_ref, o_ref, ext_ref, sh_ref, *, nt):
    j = pl.program_id(1)
    tm = cur_ref.shape[1]
    hl = CONV_HALO
    ext_ref[0:hl] = jnp.where(j > 0, prev_ref[0, tm - hl:tm, :], 0.0)
    ext_ref[hl:hl + tm] = cur_ref[0]
    ext_ref[hl + tm:2 * hl + tm] = jnp.where(j < nt - 1, next_ref[0, 0:hl, :], 0.0)
    n_sh = sh_ref.shape[1]
    for sb in range(8):
        sh_ref[sb] = ext_ref[sb:sb + n_sh, :]
    off = hl - CONV_K // 2
    for r0 in range(0, tm, CONV_ROWS):
        acc = None
        for k in range(CONV_K):
            sb, a8 = (off + k) % 8, (off + k) // 8 * 8
            term = sh_ref[sb, r0 + a8:r0 + a8 + CONV_ROWS, :] * w_ref[k:k + 1, :]
            acc = term if acc is None else acc + term
        hcv = acc + b_ref[...]
        mu = jnp.mean(hcv, axis=-1, keepdims=True)
        xc = hcv - mu
        y = xc * lax.rsqrt(jnp.mean(xc * xc, axis=-1, keepdims=True) + EPS) * lg_ref[...] + lb_ref[...]
        o_ref[0, r0:r0 + CONV_ROWS, :] = y * jax.nn.sigmoid(y)


def _conv(glu, params, l):
    b, t, ch = glu.shape
    tm = ROW_TILE
    nt = t // tm
    stacks = [params[k] for k in ("conv_w", "conv_b", "conv_ln_g", "conv_ln_b")]
    return pl.pallas_call(
        functools.partial(_conv_kernel, nt=nt),
        grid=(b, nt),
        in_specs=[
            pl.BlockSpec((1, tm, ch), lambda i, j: (i, jnp.maximum(j - 1, 0), 0)),
            pl.BlockSpec((1, tm, ch), lambda i, j: (i, j, 0)),
            pl.BlockSpec((1, tm, ch), lambda i, j: (i, jnp.minimum(j + 1, nt - 1), 0)),
        ] + [_layer_spec(a, l) for a in stacks],
        out_specs=pl.BlockSpec((1, tm, ch), lambda i, j: (i, j, 0)),
        out_shape=jax.ShapeDtypeStruct((b, t, ch), F32),
        scratch_shapes=[pltpu.VMEM((tm + 2 * CONV_HALO, ch), F32),
                        pltpu.VMEM((8, tm + 2 * CONV_HALO - 8, ch), F32)],
        compiler_params=_cparams(("arbitrary", "arbitrary")),
        name="conv",
    )(glu, glu, glu, *stacks)


def _merge_kernel(oa_ref, ob_ref, oc_ref, on_ref, wout_ref, x_ref, mod_ref, gm_ref, x1_ref, h2_ref):
    on = on_ref[...]
    na = oa_ref.shape[2]
    nb = ob_ref.shape[2]
    y = jnp.concatenate([
        _rms(oa_ref[0], on[:, 0:na]).astype(BF16),
        _rms(ob_ref[0], on[:, na:na + nb]).astype(BF16),
        _rms(oc_ref[0], on[:, na + nb:]).astype(BF16),
    ], axis=1)
    m = mod_ref[...]
    x1 = x_ref[0] + m[2:3] * _dot(y, wout_ref[...])
    x1_ref[0] = x1
    h2_ref[0] = (_rms(x1, gm_ref[...]) * (1.0 + m[4:5]) + m[3:4]).astype(BF16)


def _merge(oa, ob, oc, x, mod, mod_row, params, l):
    b, t, d = x.shape
    tm = ROW_TILE
    row = lambda i, j: (i, j, 0)
    out_norm, w_out, g_mlp = params["out_norm"], params["w_out"], params["norm_mlp"]
    return pl.pallas_call(
        _merge_kernel,
        grid=(b, t // tm),
        in_specs=[
            pl.BlockSpec((1, tm, oa.shape[2]), row),
            pl.BlockSpec((1, tm, ob.shape[2]), row),
            pl.BlockSpec((1, tm, oc.shape[2]), row),
            _layer_spec(out_norm, l),
            _layer_spec(w_out, l),
            pl.BlockSpec((1, tm, d), row),
            _mod_spec(mod, l, mod_row),
            _layer_spec(g_mlp, l),
        ],
        out_specs=(pl.BlockSpec((1, tm, d), row), pl.BlockSpec((1, tm, d), row)),
        out_shape=(jax.ShapeDtypeStruct((b, t, d), F32), jax.ShapeDtypeStruct((b, t, d), BF16)),
        compiler_params=_cparams(("arbitrary", "arbitrary")),
        name="merge",
    )(oa, ob, oc, out_norm, w_out, x, mod, g_mlp)


def _mlp_kernel(*refs, nf, final):
    if final:
        h_ref, w1_ref, w2_ref, x_ref, mod_ref, fn_ref, o_ref, acc_ref = refs
    else:
        h_ref, w1_ref, w2_ref, x_ref, mod_ref, o_ref, acc_ref = refs
    j = pl.program_id(2)

    @pl.when(j == 0)
    def _():
        acc_ref[...] = jnp.zeros_like(acc_ref)

    a = jnp.square(jnp.maximum(_dot(h_ref[0], w1_ref[...]), 0.0))
    acc_ref[...] += _dot(a.astype(BF16), w2_ref[...])

    @pl.when(j == nf - 1)
    def _():
        out = x_ref[0] + mod_ref[5:6, :] * acc_ref[...]
        if final:
            out = _rms(out, fn_ref[...])
        o_ref[0] = out


def _mlp(h2, x1, mod, mod_row, params, l, final_norm=None):
    b, t, d = x1.shape
    w1, w2 = params["w1"], params["w2"]
    dff = w1.shape[2]
    tr = min(MLP_ROWS, t)
    tf = MLP_FF_TILE
    nf = dff // tf
    final = final_norm is not None
    row = lambda i, r, j: (i, r, 0)
    in_specs = [
        pl.BlockSpec((1, tr, d), row),
        pl.BlockSpec((None, d, tf), lambda i, r, j: (l, 0, j)),
        pl.BlockSpec((None, tf, d), lambda i, r, j: (l, j, 0)),
        pl.BlockSpec((1, tr, d), row),
        _mod_spec(mod, l, mod_row),
    ]
    args = [h2, w1, w2, x1, mod]
    if final:
        in_specs.append(pl.BlockSpec(final_norm.shape, lambda i, r, j: (0, 0)))
        args.append(final_norm)
    return pl.pallas_call(
        functools.partial(_mlp_kernel, nf=nf, final=final),
        grid=(b, t // tr, nf),
        in_specs=in_specs,
        out_specs=pl.BlockSpec((1, tr, d), row),
        out_shape=jax.ShapeDtypeStruct((b, t, d), F32),
        scratch_shapes=[pltpu.VMEM((tr, d), F32)],
        compiler_params=_cparams(("arbitrary", "arbitrary", "arbitrary")),
        name="mlp_final" if final else "mlp",
    )(*args)


def _rope_tables(n_tok):
    rows = n_tok // GRID_W
    row = jnp.repeat(jnp.arange(rows, dtype=F32), GRID_W)
    col = jnp.tile(jnp.arange(GRID_W, dtype=F32), rows)
    n_freq = MLA_ROPE // 4
    inv_freq = ROPE_THETA ** (-jnp.arange(n_freq, dtype=F32) / n_freq)
    ar = row[:, None] * inv_freq
    ac = col[:, None] * inv_freq
    cos = jnp.concatenate([jnp.cos(ar), jnp.cos(ar), jnp.cos(ac), jnp.cos(ac)], axis=1)
    sin = jnp.concatenate([-jnp.sin(ar), jnp.sin(ar), -jnp.sin(ac), jnp.sin(ac)], axis=1)
    return jnp.tile(cos, (1, 2)), jnp.tile(sin, (1, 2))


def kernel(x, c, ctx, c_ctx, ada_w, ada_b, norm_mix, norm_mlp, w_in, mla_q_norm, mla_w_uq, mla_kv_norm, mla_w_ukv, swa_sink, conv_w, conv_b, conv_ln_g, conv_ln_b, out_norm, w_out, mlp_w1, mlp_w2, final_norm):
    b, s, d = x.shape
    n_ctx = ctx.shape[1]
    depth = ada_w.shape[0]
    assert s % (2 * MLA_Q_TILE) == 0 and s % MLP_ROWS == 0 and n_ctx % ROW_TILE == 0
    assert s % (SWA_STEP_BLOCKS * SWA_BLOCK) == 0
    assert s >= 3 * SWA_BLOCK and b + 1 <= 8

    cvec = jnp.concatenate([c, c_ctx[None, :], jnp.zeros((8 - b - 1, d), F32)], axis=0)
    mod = _ada(cvec, ada_w, ada_b).reshape(depth, 8, 6, d)
    rope_tabs = _rope_tables(s)

    vec = lambda v: v.reshape(depth, 1, -1)
    kr_end = OFF_KR + MLA_ROPE
    w_in_b = w_in.astype(BF16)
    params = {
        "norm_mix": vec(norm_mix),
        "w_in": jnp.concatenate(
            [w_in_b[:, :, :kr_end], jnp.zeros((depth, d, OFF_BQ - kr_end), BF16), w_in_b[:, :, kr_end:]], axis=2),
        "q_norm": vec(mla_q_norm),
        "w_uq": jnp.pad(mla_w_uq.astype(BF16), ((0, 0), (0, 0), (0, 0), (0, MLA_QK_PAD - MLA_NOPE - MLA_ROPE))
                        ).reshape(depth, MLA_RANK, MLA_HEADS * MLA_QK_PAD),
        "kv_norm": vec(mla_kv_norm),
        "w_ukv": mla_w_ukv.astype(BF16).reshape(depth, MLA_RANK, MLA_HEADS * (MLA_NOPE + MLA_V)),
        "sink": jnp.broadcast_to(swa_sink[:, :, None], (depth, SWA_HEADS, 128)),
        "conv_w": conv_w.reshape(depth, CONV_K, CONV_CH),
        "conv_b": vec(conv_b),
        "conv_ln_g": vec(conv_ln_g),
        "conv_ln_b": vec(conv_ln_b),
        "out_norm": vec(out_norm),
        "w_out": w_out.astype(BF16),
        "norm_mlp": vec(norm_mlp),
        "w1": mlp_w1.astype(BF16),
        "w2": mlp_w2.astype(BF16),
    }

    xc = ctx
    for l in range(depth):
        update_ctx = l < depth - 1
        qa, ka, va, bq, bk, bv, glu = _premix(x, mod, None, params, l, rope_tabs)
        qa_c, ka_c, va_c, bq_c, bk_c, bv_c, glu_c = _premix(xc, mod, b, params, l, None)

        oa = _mla(qa, [(ka, va), (ka_c, va_c)])
        ob = _swa(bq, bk, bv, bk_c, bv_c, params, l)
        oc = _conv(glu, params, l)
        x1, h2 = _merge(oa, ob, oc, x, mod, None, params, l)
        x = _mlp(h2, x1, mod, None, params, l, None if update_ctx else final_norm.reshape(1, d))

        if update_ctx:
            oa_c = _mla(qa_c, [(ka_c, va_c)])
            ob_c = _swa(bq_c, None, None, bk_c, bv_c, params, l)
            oc_c = _conv(glu_c, params, l)
            xc1, h2c = _merge(oa_c, ob_c, oc_c, xc, mod, b, params, l)
            flat = lambda a: a.reshape(1, b * n_ctx, d)
            xc = _mlp(flat(h2c), flat(xc1), mod, b, params, l).reshape(b, n_ctx, d)
    return x
```

```python
import functools

import jax
import jax.numpy as jnp
from jax import lax
from jax.experimental import pallas as pl
from jax.experimental.pallas import tpu as pltpu

F32 = jnp.float32
BF16 = jnp.bfloat16

EPS = 1e-6
NEG_INF = -1e30
GRID_W = 64
ROPE_THETA = 10000.0

MLA_HEADS = 8
MLA_RANK = 512
MLA_NOPE = 128
MLA_ROPE = 64
MLA_V = 128
MLA_QK_PAD = 256
LOG2E = 1.4426950408889634
MLA_Q_SCALE = (MLA_NOPE + MLA_ROPE) ** -0.5 * LOG2E
SWA_HEADS = 8
SWA_KV_HEADS = 2
SWA_GROUP = SWA_HEADS // SWA_KV_HEADS
SWA_DH = 64
SWA_WINDOW = 128
SWA_BLOCK = 128
SWA_Q_SCALE = SWA_DH ** -0.5 * LOG2E
CONV_CH = 512
CONV_K = 31
CONV_HALO = 16

OFF_AQ = 0
OFF_AKV = 512
OFF_KR = 1024
IN_PAD_WIDTH = 2944
TAIL_BQ = 0
TAIL_BK = 512
TAIL_BV = 640
TAIL_C = 768

ROW_TILE = 256
MLP_ROWS = 512
MLP_FF_TILE = 1024
MLA_Q_TILE = 256
MLA_SM_ROWS = 16
SWA_STEP_BLOCKS = 4
ADA_N_TILE = 1024
VMEM_LIMIT = 56 * 1024 * 1024


def _cparams(sem):
    return pltpu.CompilerParams(dimension_semantics=sem, vmem_limit_bytes=VMEM_LIMIT)


def _layer_spec(stack, l):
    _, a, b = stack.shape
    return pl.BlockSpec((None, a, b), lambda *_: (l, 0, 0), pipeline_mode=pl.Buffered(1))


def _mod_spec(mod, l, row):
    d = mod.shape[-1]
    return pl.BlockSpec((None, None, 6, d), lambda i, *_: (l, i if row is None else row, 0, 0))


def _rms(x, g):
    return x * lax.rsqrt(jnp.mean(x * x, axis=-1, keepdims=True) + EPS) * g


def _dot(a, b):
    return jnp.dot(a, b, preferred_element_type=F32)


def _dot_nt(a, b):
    return lax.dot_general(a, b, (((1,), (1,)), ((), ())), preferred_element_type=F32)


def _ada_kernel(c_ref, w_ref, b_ref, o_ref):
    c = c_ref[...]
    s = c * jax.nn.sigmoid(c)
    o_ref[0] = _dot(s.astype(BF16), w_ref[0].astype(BF16)) + b_ref[0]


def _ada(cvec, ada_w, ada_b):
    n_layers, d, n = ada_w.shape
    rows = cvec.shape[0]
    tn = ADA_N_TILE
    return pl.pallas_call(
        _ada_kernel,
        grid=(n_layers, n // tn),
        in_specs=[
            pl.BlockSpec((rows, d), lambda l, j: (0, 0)),
            pl.BlockSpec((1, d, tn), lambda l, j: (l, 0, j)),
            pl.BlockSpec((1, 1, tn), lambda l, j: (l, 0, j)),
        ],
        out_specs=pl.BlockSpec((1, rows, tn), lambda l, j: (l, 0, j)),
        out_shape=jax.ShapeDtypeStruct((n_layers, rows, n), F32),
        compiler_params=_cparams(("arbitrary", "arbitrary")),
        name="ada",
    )(cvec, ada_w, ada_b.reshape(n_layers, 1, n))


def _rope128(t, cos, sin):
    lane = lax.broadcasted_iota(jnp.int32, t.shape, 1)
    up = pltpu.roll(t, 128 - 16, 1)
    dn = pltpu.roll(t, 16, 1)
    sw = jnp.where((lane & 16) == 0, up, dn)
    return t * cos + sw * sin


def _premix_kernel(*refs, use_rope):
    if use_rope:
        (x_ref, mod_ref, g_ref, win_ref, qn_ref, wuq_ref, kvn_ref, wukv_ref, cos_ref, sin_ref,
         qa_ref, ka_ref, va_ref, bq_ref, bk_ref, bv_ref, glu_ref) = refs
        cos = cos_ref[...]
        sin = sin_ref[...]
        rope = lambda t: _rope128(t, cos, sin)
    else:
        (x_ref, mod_ref, g_ref, win_ref, qn_ref, wuq_ref, kvn_ref, wukv_ref,
         qa_ref, ka_ref, va_ref, bq_ref, bk_ref, bv_ref, glu_ref) = refs
        rope = lambda t: t

    m = mod_ref[...]
    h = _rms(x_ref[0], g_ref[...]) * (1.0 + m[1:2]) + m[0:1]
    p = _dot(h.astype(BF16), win_ref[...])
    tail = pltpu.roll(p[:, OFF_KR:], IN_PAD_WIDTH - OFF_KR - MLA_ROPE, 1)
    lane = lax.broadcasted_iota(jnp.int32, (p.shape[0], 128), 1)
    kr = jnp.where(lane < MLA_ROPE, p[:, OFF_KR:OFF_KR + 128], 0.0)

    qn = _rms(p[:, OFF_AQ:OFF_AQ + MLA_RANK], qn_ref[...])
    q = _dot(qn.astype(BF16), wuq_ref[...])
    for hd in range(MLA_HEADS):
        c0 = hd * MLA_QK_PAD
        qa_ref[0, hd, :, 0:128] = (q[:, c0:c0 + 128] * MLA_Q_SCALE).astype(BF16)
        qa_ref[0, hd, :, 128:256] = (rope(q[:, c0 + 128:c0 + 256]) * MLA_Q_SCALE).astype(BF16)

    kvn = _rms(p[:, OFF_AKV:OFF_AKV + MLA_RANK], kvn_ref[...])
    kv = _dot(kvn.astype(BF16), wukv_ref[...])
    kpe = rope(kr).astype(BF16)
    for hd in range(MLA_HEADS):
        c0 = hd * (MLA_NOPE + MLA_V)
        ka_ref[0, hd, :, 0:128] = kv[:, c0:c0 + 128].astype(BF16)
        ka_ref[0, hd, :, 128:256] = kpe
        va_ref[0, hd] = kv[:, c0 + 128:c0 + 256].astype(BF16)

    for t in range(SWA_HEADS * SWA_DH // 128):
        c0 = TAIL_BQ + t * 128
        bq_ref[0, :, t * 128:(t + 1) * 128] = (rope(tail[:, c0:c0 + 128]) * SWA_Q_SCALE).astype(BF16)
    bk_ref[0] = rope(tail[:, TAIL_BK:TAIL_BK + 128]).astype(BF16)
    bv_ref[0] = tail[:, TAIL_BV:TAIL_BV + 128].astype(BF16)

    glu_ref[0] = (tail[:, TAIL_C:TAIL_C + CONV_CH]
                  * jax.nn.sigmoid(tail[:, TAIL_C + CONV_CH:TAIL_C + 2 * CONV_CH]))


def _premix(x, mod, mod_row, params, l, rope_tabs):
    b, t, d = x.shape
    tm = ROW_TILE
    use_rope = rope_tabs is not None
    stacks = [params[k] for k in ("norm_mix", "w_in", "q_norm", "w_uq", "kv_norm", "w_ukv")]
    in_specs = [pl.BlockSpec((1, tm, d), lambda i, j: (i, j, 0)), _mod_spec(mod, l, mod_row)]
    in_specs += [_layer_spec(a, l) for a in stacks]
    args = [x, mod] + stacks
    if use_rope:
        in_specs += [pl.BlockSpec((tm, 128), lambda i, j: (j, 0))] * 2
        args += list(rope_tabs)
    hq = MLA_HEADS
    out_shape = (
        jax.ShapeDtypeStruct((b, hq, t, MLA_QK_PAD), BF16),
        jax.ShapeDtypeStruct((b, hq, t, MLA_QK_PAD), BF16),
        jax.ShapeDtypeStruct((b, hq, t, MLA_V), BF16),
        jax.ShapeDtypeStruct((b, t, SWA_HEADS * SWA_DH), BF16),
        jax.ShapeDtypeStruct((b, t, SWA_KV_HEADS * SWA_DH), BF16),
        jax.ShapeDtypeStruct((b, t, SWA_KV_HEADS * SWA_DH), BF16),
        jax.ShapeDtypeStruct((b, t, CONV_CH), F32),
    )
    out_specs = (
        pl.BlockSpec((1, hq, tm, MLA_QK_PAD), lambda i, j: (i, 0, j, 0)),
        pl.BlockSpec((1, hq, tm, MLA_QK_PAD), lambda i, j: (i, 0, j, 0)),
        pl.BlockSpec((1, hq, tm, MLA_V), lambda i, j: (i, 0, j, 0)),
        pl.BlockSpec((1, tm, SWA_HEADS * SWA_DH), lambda i, j: (i, j, 0)),
        pl.BlockSpec((1, tm, SWA_KV_HEADS * SWA_DH), lambda i, j: (i, j, 0)),
        pl.BlockSpec((1, tm, SWA_KV_HEADS * SWA_DH), lambda i, j: (i, j, 0)),
        pl.BlockSpec((1, tm, CONV_CH), lambda i, j: (i, j, 0)),
    )
    return pl.pallas_call(
        functools.partial(_premix_kernel, use_rope=use_rope),
        grid=(b, t // tm),
        in_specs=in_specs,
        out_specs=out_specs,
        out_shape=out_shape,
        compiler_params=_cparams(("arbitrary", "arbitrary")),
        name="premix_rope" if use_rope else "premix",
    )(*args)


def _mla_kernel(*refs, src_lens, tile):
    n_src = len(src_lens)
    q_ref = refs[0]
    kv_refs = refs[1:1 + 2 * n_src]
    o_ref = refs[1 + 2 * n_src]
    s_refs = refs[2 + 2 * n_src:4 + 2 * n_src]
    p_refs = refs[4 + 2 * n_src:6 + 2 * n_src]
    l_refs = refs[6 + 2 * n_src:8 + 2 * n_src]
    offs = [sum(src_lens[:i]) for i in range(n_src)]
    assert q_ref.shape[2] == 2 * tile

    def rows(t):
        return slice(t * tile, (t + 1) * tile)

    def scores(t):
        qt = q_ref[0, 0, rows(t), :]
        for si in range(n_src):
            s_refs[t][:, offs[si]:offs[si] + src_lens[si]] = _dot_nt(qt, kv_refs[2 * si][0, 0])

    def softmax(t):
        for r in range(0, tile, MLA_SM_ROWS):
            rs = slice(r, r + MLA_SM_ROWS)
            sb = s_refs[t][rs, :]
            p = jnp.exp2(sb - jnp.max(sb, axis=1, keepdims=True))
            l_refs[t][rs, :] = jnp.broadcast_to(jnp.sum(p, axis=1, keepdims=True), (MLA_SM_ROWS, 128))
            p_refs[t][rs, :] = p.astype(BF16)

    def values(t):
        acc = None
        for si in range(n_src):
            part = _dot(p_refs[t][:, offs[si]:offs[si] + src_lens[si]], kv_refs[2 * si + 1][0, 0])
            acc = part if acc is None else acc + part
        o_ref[0, rows(t), :] = acc / l_refs[t][...]

    scores(0)
    scores(1)
    softmax(0)
    values(0)
    softmax(1)
    values(1)


def _mla(q, kv_sources):
    b, hq, tq_all, dq = q.shape
    tile = min(MLA_Q_TILE, tq_all // 2)
    tq = 2 * tile
    assert tq_all % tq == 0
    in_specs = [pl.BlockSpec((1, 1, tq, dq), lambda i, h, j: (i, h, j, 0))]
    args = [q]
    src_lens = []
    for k, v in kv_sources:
        n = k.shape[2]
        src_lens.append(n)
        in_specs.append(pl.BlockSpec((1, 1, n, dq), lambda i, h, j: (i, h, 0, 0)))
        in_specs.append(pl.BlockSpec((1, 1, n, MLA_V), lambda i, h, j: (i, h, 0, 0)))
        args += [k, v]
    n_keys = sum(src_lens)
    return pl.pallas_call(
        functools.partial(_mla_kernel, src_lens=tuple(src_lens), tile=tile),
        grid=(b, hq, tq_all // tq),
        in_specs=in_specs,
        out_specs=pl.BlockSpec((1, tq, MLA_V), lambda i, h, j: (i, j, h)),
        out_shape=jax.ShapeDtypeStruct((b, tq_all, hq * MLA_V), F32),
        scratch_shapes=(
            [pltpu.VMEM((tile, n_keys), F32)] * 2
            + [pltpu.VMEM((tile, n_keys), BF16)] * 2
            + [pltpu.VMEM((tile, 128), F32)] * 2),
        compiler_params=_cparams(("arbitrary", "arbitrary", "arbitrary")),
        name="mla_%d" % len(kv_sources),
    )(*args)


def _swa_kernel(*refs, s_len, nblk):
    latent = s_len > 0
    if latent:
        q_ref, kl_ref, vl_ref, kc_ref, vc_ref, sink_ref, o_ref = refs
    else:
        q_ref, kc_ref, vc_ref, sink_ref, o_ref = refs
    kc = kc_ref[0]
    vc = vc_ref[0]
    rows = SWA_GROUP * SWA_BLOCK
    win = 3 * SWA_BLOCK
    chains = [(blk, kh) for blk in range(nblk) for kh in range(SWA_KV_HEADS)]
    kcat, vcat, valid = {}, {}, {}
    n_win_tiles = win // 128 if latent else 0
    if latent:
        d = (lax.broadcasted_iota(jnp.int32, (rows, win), 1)
             - (lax.broadcasted_iota(jnp.int32, (rows, win), 0) & (SWA_BLOCK - 1)))
        for blk in range(nblk):
            n = pl.program_id(1) * nblk + blk
            start = jnp.clip((n - 1) * SWA_BLOCK, 0, s_len - win)
            start = pl.multiple_of(start, SWA_BLOCK)
            kcat[blk] = jnp.concatenate([kl_ref[0, pl.ds(start, win), :], kc], axis=0)
            vcat[blk] = jnp.concatenate([vl_ref[0, pl.ds(start, win), :], vc], axis=0)
            valid[blk] = jnp.abs(d + (start - n * SWA_BLOCK)) <= SWA_WINDOW
    else:
        for blk in range(nblk):
            kcat[blk], vcat[blk] = kc, vc

    s, snk, mx = {}, {}, {}
    for c in chains:
        blk, kh = c
        heads = range(kh * SWA_GROUP, (kh + 1) * SWA_GROUP)
        lo, hi = kh * SWA_DH, (kh + 1) * SWA_DH
        q = q_ref[0, blk * SWA_BLOCK:(blk + 1) * SWA_BLOCK, :]
        qs = jnp.concatenate([q[:, h * SWA_DH:(h + 1) * SWA_DH] for h in heads], axis=0)
        snk[c] = jnp.concatenate(
            [jnp.broadcast_to(sink_ref[h:h + 1, 0:1] * LOG2E, (SWA_BLOCK, 1)) for h in heads], axis=0)
        sc = _dot_nt(qs, kcat[blk][:, lo:hi])
        tiles = [sc[:, j * 128:(j + 1) * 128] for j in range(sc.shape[1] // 128)]
        for j in range(n_win_tiles):
            tiles[j] = jnp.where(valid[blk][:, j * 128:(j + 1) * 128], tiles[j], NEG_INF)
        s[c] = tiles
    for c in chains:
        mm = functools.reduce(jnp.maximum, s[c])
        mx[c] = jnp.maximum(jnp.max(mm, axis=1, keepdims=True), snk[c])
    prob, den = {}, {}
    for c in chains:
        ps = [jnp.exp2(t - mx[c]) for t in s[c]]
        den[c] = jnp.sum(functools.reduce(jnp.add, ps), axis=1, keepdims=True) + jnp.exp2(snk[c] - mx[c])
        prob[c] = jnp.concatenate([p.astype(BF16) for p in ps], axis=1)
    for c in chains:
        blk, kh = c
        lo, hi = kh * SWA_DH, (kh + 1) * SWA_DH
        o = _dot(prob[c], vcat[blk][:, lo:hi]) / den[c]
        for g in range(SWA_GROUP):
            h = kh * SWA_GROUP + g
            o_ref[0, blk * SWA_BLOCK:(blk + 1) * SWA_BLOCK, h * SWA_DH:(h + 1) * SWA_DH] = (
                o[g * SWA_BLOCK:(g + 1) * SWA_BLOCK])


def _swa(q, k_lat, v_lat, k_ctx, v_ctx, params, l):
    sink_b = params["sink"]
    b, tq_all, dq = q.shape
    dkv = k_ctx.shape[2]
    n_ctx = k_ctx.shape[1]
    latent = k_lat is not None
    nblk = min(SWA_STEP_BLOCKS, tq_all // SWA_BLOCK)
    tq = nblk * SWA_BLOCK
    in_specs = [pl.BlockSpec((1, tq, dq), lambda i, j: (i, j, 0))]
    args = [q]
    if latent:
        s_len = k_lat.shape[1]
        in_specs += [pl.BlockSpec((1, s_len, dkv), lambda i, j: (i, 0, 0))] * 2
        args += [k_lat, v_lat]
    else:
        s_len = 0
    in_specs += [pl.BlockSpec((1, n_ctx, dkv), lambda i, j: (i, 0, 0))] * 2
    in_specs += [_layer_spec(sink_b, l)]
    args += [k_ctx, v_ctx, sink_b]
    return pl.pallas_call(
        functools.partial(_swa_kernel, s_len=s_len, nblk=nblk),
        grid=(b, tq_all // tq),
        in_specs=in_specs,
        out_specs=pl.BlockSpec((1, tq, dq), lambda i, j: (i, j, 0)),
        out_shape=jax.ShapeDtypeStruct((b, tq_all, dq), F32),
        compiler_params=_cparams(("arbitrary", "arbitrary")),
        name="swa_lat" if latent else "swa_ctx",
    )(*args)


CONV_ROWS = 32


def _conv_rows(prev_ref, cur_ref, next_ref, w_ref, b_ref, lg_ref, lb_ref, ext_ref, sh_ref, nt):
    j = pl.program_id(1)
    tm = cur_ref.shape[1]
    hl = CONV_HALO
    ext_ref[0:hl] = jnp.where(j > 0, prev_ref[0, tm - hl:tm, :], 0.0)
    ext_ref[hl:hl + tm] = cur_ref[0]
    ext_ref[hl + tm:2 * hl + tm] = jnp.where(j < nt - 1, next_ref[0, 0:hl, :], 0.0)
    n_sh = sh_ref.shape[1]
    for sb in range(8):
        sh_ref[sb] = ext_ref[sb:sb + n_sh, :]
    off = hl - CONV_K // 2
    for r0 in range(0, tm, CONV_ROWS):
        acc = None
        for k in range(CONV_K):
            sb, a8 = (off + k) % 8, (off + k) // 8 * 8
            term = sh_ref[sb, r0 + a8:r0 + a8 + CONV_ROWS, :] * w_ref[k:k + 1, :]
            acc = term if acc is None else acc + term
        hcv = acc + b_ref[...]
        mu = jnp.mean(hcv, axis=-1, keepdims=True)
        xc = hcv - mu
        y = xc * lax.rsqrt(jnp.mean(xc * xc, axis=-1, keepdims=True) + EPS) * lg_ref[...] + lb_ref[...]
        yield r0, y * jax.nn.sigmoid(y)


def _merge_kernel(oa_ref, ob_ref, gp_ref, gc_ref, gn_ref, cw_ref, cb_ref, lg_ref, lb_ref,
                  on_ref, wout_ref, x_ref, mod_ref, gm_ref, x1_ref, h2_ref, ext_ref, sh_ref, yc_ref, *, nt):
    on = on_ref[...]
    na = oa_ref.shape[2]
    nb = ob_ref.shape[2]
    yab = jnp.concatenate([_rms(oa_ref[0], on[:, 0:na]), _rms(ob_ref[0], on[:, na:na + nb])],
                          axis=1).astype(BF16)
    proj = _dot(yab, wout_ref[0:na + nb, :])
    for r0, oc in _conv_rows(gp_ref, gc_ref, gn_ref, cw_ref, cb_ref, lg_ref, lb_ref, ext_ref, sh_ref, nt):
        yc_ref[r0:r0 + CONV_ROWS, :] = _rms(oc, on[:, na + nb:]).astype(BF16)
    proj = proj + _dot(yc_ref[...], wout_ref[na + nb:, :])
    m = mod_ref[...]
    x1 = x_ref[0] + m[2:3] * proj
    x1_ref[0] = x1
    h2_ref[0] = (_rms(x1, gm_ref[...]) * (1.0 + m[4:5]) + m[3:4]).astype(BF16)


def _merge(oa, ob, glu, x, mod, mod_row, params, l):
    b, t, d = x.shape
    ch = glu.shape[2]
    tm = ROW_TILE
    nt = t // tm
    row = lambda i, j: (i, j, 0)
    stacks = [params[k] for k in ("conv_w", "conv_b", "conv_ln_g", "conv_ln_b", "out_norm", "w_out")]
    g_mlp = params["norm_mlp"]
    return pl.pallas_call(
        functools.partial(_merge_kernel, nt=nt),
        grid=(b, nt),
        in_specs=[
            pl.BlockSpec((1, tm, oa.shape[2]), row),
            pl.BlockSpec((1, tm, ob.shape[2]), row),
            pl.BlockSpec((1, tm, ch), lambda i, j: (i, jnp.maximum(j - 1, 0), 0)),
            pl.BlockSpec((1, tm, ch), row),
            pl.BlockSpec((1, tm, ch), lambda i, j: (i, jnp.minimum(j + 1, nt - 1), 0)),
        ] + [_layer_spec(a, l) for a in stacks] + [
            pl.BlockSpec((1, tm, d), row),
            _mod_spec(mod, l, mod_row),
            _layer_spec(g_mlp, l),
        ],
        out_specs=(pl.BlockSpec((1, tm, d), row), pl.BlockSpec((1, tm, d), row)),
        out_shape=(jax.ShapeDtypeStruct((b, t, d), F32), jax.ShapeDtypeStruct((b, t, d), BF16)),
        scratch_shapes=[pltpu.VMEM((tm + 2 * CONV_HALO, ch), F32),
                        pltpu.VMEM((8, tm + 2 * CONV_HALO - 8, ch), F32),
                        pltpu.VMEM((tm, ch), BF16)],
        compiler_params=_cparams(("arbitrary", "arbitrary")),
        name="merge",
    )(oa, ob, glu, glu, glu, *stacks, x, mod, g_mlp)


def _mlp_kernel(*refs, nf, final):
    if final:
        h_ref, w1_ref, w2_ref, x_ref, mod_ref, fn_ref, o_ref, acc_ref = refs
    else:
        h_ref, w1_ref, w2_ref, x_ref, mod_ref, o_ref, acc_ref = refs
    j = pl.program_id(2)

    @pl.when(j == 0)
    def _():
        acc_ref[...] = jnp.zeros_like(acc_ref)

    a = jnp.square(jnp.maximum(_dot(h_ref[0], w1_ref[...]), 0.0))
    acc_ref[...] += _dot(a.astype(BF16), w2_ref[...])

    @pl.when(j == nf - 1)
    def _():
        out = x_ref[0] + mod_ref[5:6, :] * acc_ref[...]
        if final:
            out = _rms(out, fn_ref[...])
        o_ref[0] = out


def _mlp(h2, x1, mod, mod_row, params, l, final_norm=None):
    b, t, d = x1.shape
    w1, w2 = params["w1"], params["w2"]
    dff = w1.shape[2]
    tr = min(MLP_ROWS, t)
    tf = MLP_FF_TILE
    nf = dff // tf
    final = final_norm is not None
    row = lambda i, r, j: (i, r, 0)
    in_specs = [
        pl.BlockSpec((1, tr, d), row),
        pl.BlockSpec((None, d, tf), lambda i, r, j: (l, 0, j)),
        pl.BlockSpec((None, tf, d), lambda i, r, j: (l, j, 0)),
        pl.BlockSpec((1, tr, d), row),
        _mod_spec(mod, l, mod_row),
    ]
    args = [h2, w1, w2, x1, mod]
    if final:
        in_specs.append(pl.BlockSpec(final_norm.shape, lambda i, r, j: (0, 0)))
        args.append(final_norm)
    return pl.pallas_call(
        functools.partial(_mlp_kernel, nf=nf, final=final),
        grid=(b, t // tr, nf),
        in_specs=in_specs,
        out_specs=pl.BlockSpec((1, tr, d), row),
        out_shape=jax.ShapeDtypeStruct((b, t, d), F32),
        scratch_shapes=[pltpu.VMEM((tr, d), F32)],
        compiler_params=_cparams(("arbitrary", "arbitrary", "arbitrary")),
        name="mlp_final" if final else "mlp",
    )(*args)


def _rope_tables(n_tok):
    rows = n_tok // GRID_W
    row = jnp.repeat(jnp.arange(rows, dtype=F32), GRID_W)
    col = jnp.tile(jnp.arange(GRID_W, dtype=F32), rows)
    n_freq = MLA_ROPE // 4
    inv_freq = ROPE_THETA ** (-jnp.arange(n_freq, dtype=F32) / n_freq)
    ar = row[:, None] * inv_freq
    ac = col[:, None] * inv_freq
    cos = jnp.concatenate([jnp.cos(ar), jnp.cos(ar), jnp.cos(ac), jnp.cos(ac)], axis=1)
    sin = jnp.concatenate([-jnp.sin(ar), jnp.sin(ar), -jnp.sin(ac), jnp.sin(ac)], axis=1)
    return jnp.tile(cos, (1, 2)), jnp.tile(sin, (1, 2))


def kernel(x, c, ctx, c_ctx, ada_w, ada_b, norm_mix, norm_mlp, w_in, mla_q_norm, mla_w_uq, mla_kv_norm, mla_w_ukv, swa_sink, conv_w, conv_b, conv_ln_g, conv_ln_b, out_norm, w_out, mlp_w1, mlp_w2, final_norm):
    b, s, d = x.shape
    n_ctx = ctx.shape[1]
    depth = ada_w.shape[0]
    assert s % (2 * MLA_Q_TILE) == 0 and s % MLP_ROWS == 0 and n_ctx % ROW_TILE == 0
    assert s % (SWA_STEP_BLOCKS * SWA_BLOCK) == 0
    assert s >= 3 * SWA_BLOCK and b + 1 <= 8

    cvec = jnp.concatenate([c, c_ctx[None, :], jnp.zeros((8 - b - 1, d), F32)], axis=0)
    mod = _ada(cvec, ada_w, ada_b).reshape(depth, 8, 6, d)
    rope_tabs = _rope_tables(s)

    vec = lambda v: v.reshape(depth, 1, -1)
    params = {
        "norm_mix": vec(norm_mix),
        "w_in": jnp.pad(w_in.astype(BF16), ((0, 0), (0, 0), (0, IN_PAD_WIDTH - w_in.shape[2]))),
        "q_norm": vec(mla_q_norm),
        "w_uq": jnp.pad(mla_w_uq.astype(BF16), ((0, 0), (0, 0), (0, 0), (0, MLA_QK_PAD - MLA_NOPE - MLA_ROPE))
                        ).reshape(depth, MLA_RANK, MLA_HEADS * MLA_QK_PAD),
        "kv_norm": vec(mla_kv_norm),
        "w_ukv": mla_w_ukv.astype(BF16).reshape(depth, MLA_RANK, MLA_HEADS * (MLA_NOPE + MLA_V)),
        "sink": jnp.broadcast_to(swa_sink[:, :, None], (depth, SWA_HEADS, 128)),
        "conv_w": conv_w.reshape(depth, CONV_K, CONV_CH),
        "conv_b": vec(conv_b),
        "conv_ln_g": vec(conv_ln_g),
        "conv_ln_b": vec(conv_ln_b),
        "out_norm": vec(out_norm),
        "w_out": w_out.astype(BF16),
        "norm_mlp": vec(norm_mlp),
        "w1": mlp_w1.astype(BF16),
        "w2": mlp_w2.astype(BF16),
    }

    xc = ctx
    for l in range(depth):
        update_ctx = l < depth - 1
        qa, ka, va, bq, bk, bv, glu = _premix(x, mod, None, params, l, rope_tabs)
        qa_c, ka_c, va_c, bq_c, bk_c, bv_c, glu_c = _premix(xc, mod, b, params, l, None)

        oa = _mla(qa, [(ka, va), (ka_c, va_c)])
        ob = _swa(bq, bk, bv, bk_c, bv_c, params, l)
        x1, h2 = _merge(oa, ob, glu, x, mod, None, params, l)
        x = _mlp(h2, x1, mod, None, params, l, None if update_ctx else final_norm.reshape(1, d))

        if update_ctx:
            oa_c = _mla(qa_c, [(ka_c, va_c)])
            ob_c = _swa(bq_c, None, None, bk_c, bv_c, params, l)
            xc1, h2c = _merge(oa_c, ob_c, glu_c, xc, mod, b, params, l)
            flat = lambda a: a.reshape(1, b * n_ctx, d)
            xc = _mlp(flat(h2c), flat(xc1), mod, b, params, l).reshape(b, n_ctx, d)
    return x
```

```python
import functools

import jax
import jax.numpy as jnp
from jax import lax
from jax.experimental import pallas as pl
from jax.experimental.pallas import tpu as pltpu

F32 = jnp.float32
BF16 = jnp.bfloat16

EPS = 1e-6
NEG_INF = -1e30
GRID_W = 64
ROPE_THETA = 10000.0

MLA_HEADS = 8
MLA_RANK = 512
MLA_NOPE = 128
MLA_ROPE = 64
MLA_V = 128
MLA_QK_PAD = 256
LOG2E = 1.4426950408889634
MLA_Q_SCALE = (MLA_NOPE + MLA_ROPE) ** -0.5 * LOG2E
SWA_HEADS = 8
SWA_KV_HEADS = 2
SWA_GROUP = SWA_HEADS // SWA_KV_HEADS
SWA_DH = 64
SWA_WINDOW = 128
SWA_BLOCK = 128
SWA_Q_SCALE = SWA_DH ** -0.5 * LOG2E
CONV_CH = 512
CONV_K = 31
CONV_HALO = 16

OFF_AQ = 0
OFF_AKV = 512
OFF_KR = 1024
IN_PAD_WIDTH = 2944
TAIL_BQ = 0
TAIL_BK = 512
TAIL_BV = 640
TAIL_C = 768

ROW_TILE = 256
MLP_ROWS = 512
MLP_FF_TILE = 1024
MLA_Q_TILE = 256
MLA_STEP_ROWS = 2048
MLA_SM_ROWS = 16
SWA_STEP_BLOCKS = 4
ADA_N_TILE = 1024
VMEM_LIMIT = 56 * 1024 * 1024


def _cparams(sem):
    return pltpu.CompilerParams(dimension_semantics=sem, vmem_limit_bytes=VMEM_LIMIT)


def _layer_spec(stack, l):
    _, a, b = stack.shape
    return pl.BlockSpec((None, a, b), lambda *_: (l, 0, 0), pipeline_mode=pl.Buffered(1))


def _mod_spec(mod, l, row):
    d = mod.shape[-1]
    return pl.BlockSpec((None, None, 6, d), lambda i, *_: (l, i if row is None else row, 0, 0))


def _rms(x, g):
    return x * lax.rsqrt(jnp.mean(x * x, axis=-1, keepdims=True) + EPS) * g


def _dot(a, b):
    return jnp.dot(a, b, preferred_element_type=F32)


def _dot_nt(a, b):
    return lax.dot_general(a, b, (((1,), (1,)), ((), ())), preferred_element_type=F32)


def _ada_kernel(c_ref, w_ref, b_ref, o_ref):
    c = c_ref[...]
    s = c * jax.nn.sigmoid(c)
    o_ref[0] = _dot(s.astype(BF16), w_ref[0].astype(BF16)) + b_ref[0]


def _ada(cvec, ada_w, ada_b):
    n_layers, d, n = ada_w.shape
    rows = cvec.shape[0]
    tn = ADA_N_TILE
    return pl.pallas_call(
        _ada_kernel,
        grid=(n_layers, n // tn),
        in_specs=[
            pl.BlockSpec((rows, d), lambda l, j: (0, 0)),
            pl.BlockSpec((1, d, tn), lambda l, j: (l, 0, j)),
            pl.BlockSpec((1, 1, tn), lambda l, j: (l, 0, j)),
        ],
        out_specs=pl.BlockSpec((1, rows, tn), lambda l, j: (l, 0, j)),
        out_shape=jax.ShapeDtypeStruct((n_layers, rows, n), F32),
        compiler_params=_cparams(("arbitrary", "arbitrary")),
        name="ada",
    )(cvec, ada_w, ada_b.reshape(n_layers, 1, n))


def _rope128(t, cos, sin):
    lane = lax.broadcasted_iota(jnp.int32, t.shape, 1)
    up = pltpu.roll(t, 128 - 16, 1)
    dn = pltpu.roll(t, 16, 1)
    sw = jnp.where((lane & 16) == 0, up, dn)
    return t * cos + sw * sin


def _premix_kernel(*refs, use_rope):
    if use_rope:
        (x_ref, mod_ref, g_ref, win_ref, qn_ref, wuq_ref, kvn_ref, wukv_ref, cos_ref, sin_ref,
         qa_ref, ka_ref, va_ref, bq_ref, bk_ref, bv_ref, glu_ref) = refs
        cos = cos_ref[...]
        sin = sin_ref[...]
        rope = lambda t: _rope128(t, cos, sin)
    else:
        (x_ref, mod_ref, g_ref, win_ref, qn_ref, wuq_ref, kvn_ref, wukv_ref,
         qa_ref, ka_ref, va_ref, bq_ref, bk_ref, bv_ref, glu_ref) = refs
        rope = lambda t: t

    m = mod_ref[...]
    h = _rms(x_ref[0], g_ref[...]) * (1.0 + m[1:2]) + m[0:1]
    p = _dot(h.astype(BF16), win_ref[...])
    tail = pltpu.roll(p[:, OFF_KR:], IN_PAD_WIDTH - OFF_KR - MLA_ROPE, 1)
    lane = lax.broadcasted_iota(jnp.int32, (p.shape[0], 128), 1)
    kr = jnp.where(lane < MLA_ROPE, p[:, OFF_KR:OFF_KR + 128], 0.0)

    qn = _rms(p[:, OFF_AQ:OFF_AQ + MLA_RANK], qn_ref[...])
    q = _dot(qn.astype(BF16), wuq_ref[...])
    for hd in range(MLA_HEADS):
        c0 = hd * MLA_QK_PAD
        qa_ref[0, hd, :, 0:128] = (q[:, c0:c0 + 128] * MLA_Q_SCALE).astype(BF16)
        qa_ref[0, hd, :, 128:256] = (rope(q[:, c0 + 128:c0 + 256]) * MLA_Q_SCALE).astype(BF16)

    kvn = _rms(p[:, OFF_AKV:OFF_AKV + MLA_RANK], kvn_ref[...])
    kv = _dot(kvn.astype(BF16), wukv_ref[...])
    kpe = rope(kr).astype(BF16)
    for hd in range(MLA_HEADS):
        c0 = hd * (MLA_NOPE + MLA_V)
        ka_ref[0, hd, :, 0:128] = kv[:, c0:c0 + 128].astype(BF16)
        ka_ref[0, hd, :, 128:256] = kpe
        va_ref[0, hd] = kv[:, c0 + 128:c0 + 256].astype(BF16)

    for t in range(SWA_HEADS * SWA_DH // 128):
        c0 = TAIL_BQ + t * 128
        bq_ref[0, :, t * 128:(t + 1) * 128] = (rope(tail[:, c0:c0 + 128]) * SWA_Q_SCALE).astype(BF16)
    bk_ref[0] = rope(tail[:, TAIL_BK:TAIL_BK + 128]).astype(BF16)
    bv_ref[0] = tail[:, TAIL_BV:TAIL_BV + 128].astype(BF16)

    glu_ref[0] = (tail[:, TAIL_C:TAIL_C + CONV_CH]
                  * jax.nn.sigmoid(tail[:, TAIL_C + CONV_CH:TAIL_C + 2 * CONV_CH]))


def _premix(x, mod, mod_row, params, l, rope_tabs):
    b, t, d = x.shape
    tm = ROW_TILE
    use_rope = rope_tabs is not None
    stacks = [params[k] for k in ("norm_mix", "w_in", "q_norm", "w_uq", "kv_norm", "w_ukv")]
    in_specs = [pl.BlockSpec((1, tm, d), lambda i, j: (i, j, 0)), _mod_spec(mod, l, mod_row)]
    in_specs += [_layer_spec(a, l) for a in stacks]
    args = [x, mod] + stacks
    if use_rope:
        in_specs += [pl.BlockSpec((tm, 128), lambda i, j: (j, 0))] * 2
        args += list(rope_tabs)
    hq = MLA_HEADS
    out_shape = (
        jax.ShapeDtypeStruct((b, hq, t, MLA_QK_PAD), BF16),
        jax.ShapeDtypeStruct((b, hq, t, MLA_QK_PAD), BF16),
        jax.ShapeDtypeStruct((b, hq, t, MLA_V), BF16),
        jax.ShapeDtypeStruct((b, t, SWA_HEADS * SWA_DH), BF16),
        jax.ShapeDtypeStruct((b, t, SWA_KV_HEADS * SWA_DH), BF16),
        jax.ShapeDtypeStruct((b, t, SWA_KV_HEADS * SWA_DH), BF16),
        jax.ShapeDtypeStruct((b, t, CONV_CH), F32),
    )
    out_specs = (
        pl.BlockSpec((1, hq, tm, MLA_QK_PAD), lambda i, j: (i, 0, j, 0)),
        pl.BlockSpec((1, hq, tm, MLA_QK_PAD), lambda i, j: (i, 0, j, 0)),
        pl.BlockSpec((1, hq, tm, MLA_V), lambda i, j: (i, 0, j, 0)),
        pl.BlockSpec((1, tm, SWA_HEADS * SWA_DH), lambda i, j: (i, j, 0)),
        pl.BlockSpec((1, tm, SWA_KV_HEADS * SWA_DH), lambda i, j: (i, j, 0)),
        pl.BlockSpec((1, tm, SWA_KV_HEADS * SWA_DH), lambda i, j: (i, j, 0)),
        pl.BlockSpec((1, tm, CONV_CH), lambda i, j: (i, j, 0)),
    )
    return pl.pallas_call(
        functools.partial(_premix_kernel, use_rope=use_rope),
        grid=(b, t // tm),
        in_specs=in_specs,
        out_specs=out_specs,
        out_shape=out_shape,
        compiler_params=_cparams(("arbitrary", "arbitrary")),
        name="premix_rope" if use_rope else "premix",
    )(*args)


def _mla_kernel(*refs, src_lens, tile):
    n_src = len(src_lens)
    q_ref = refs[0]
    kv_refs = refs[1:1 + 2 * n_src]
    o_ref = refs[1 + 2 * n_src]
    s_refs = refs[2 + 2 * n_src:4 + 2 * n_src]
    p_refs = refs[4 + 2 * n_src:6 + 2 * n_src]
    l_refs = refs[6 + 2 * n_src:8 + 2 * n_src]
    offs = [sum(src_lens[:i]) for i in range(n_src)]
    n_pairs = q_ref.shape[2] // (2 * tile)

    def pair(pr):
        def rows(t):
            start = (2 * pr + t) * tile
            return pl.ds(start if isinstance(pr, int) else pl.multiple_of(start, tile), tile)

        _mla_pair(q_ref, kv_refs, o_ref, s_refs, p_refs, l_refs, rows, src_lens, offs, tile)

    if n_pairs == 1:
        pair(0)
    else:
        pl.loop(0, n_pairs)(pair)


def _mla_pair(q_ref, kv_refs, o_ref, s_refs, p_refs, l_refs, rows, src_lens, offs, tile):
    n_src = len(src_lens)

    def scores(t):
        qt = q_ref[0, 0, rows(t), :]
        for si in range(n_src):
            s_refs[t][:, offs[si]:offs[si] + src_lens[si]] = _dot_nt(qt, kv_refs[2 * si][0, 0])

    def softmax(t):
        for r in range(0, tile, MLA_SM_ROWS):
            rs = slice(r, r + MLA_SM_ROWS)
            sb = s_refs[t][rs, :]
            p = jnp.exp2(sb - jnp.max(sb, axis=1, keepdims=True))
            l_refs[t][rs, :] = jnp.broadcast_to(jnp.sum(p, axis=1, keepdims=True), (MLA_SM_ROWS, 128))
            p_refs[t][rs, :] = p.astype(BF16)

    def values(t):
        acc = None
        for si in range(n_src):
            part = _dot(p_refs[t][:, offs[si]:offs[si] + src_lens[si]], kv_refs[2 * si + 1][0, 0])
            acc = part if acc is None else acc + part
        o_ref[0, rows(t), :] = acc / l_refs[t][...]

    scores(0)
    scores(1)
    softmax(0)
    values(0)
    softmax(1)
    values(1)


def _mla(q, kv_sources):
    b, hq, tq_all, dq = q.shape
    tile = min(MLA_Q_TILE, tq_all // 2)
    tq = min(MLA_STEP_ROWS, tq_all)
    assert tq_all % tq == 0 and tq % (2 * tile) == 0
    in_specs = [pl.BlockSpec((1, 1, tq, dq), lambda i, h, j: (i, h, j, 0))]
    args = [q]
    src_lens = []
    for k, v in kv_sources:
        n = k.shape[2]
        src_lens.append(n)
        in_specs.append(pl.BlockSpec((1, 1, n, dq), lambda i, h, j: (i, h, 0, 0)))
        in_specs.append(pl.BlockSpec((1, 1, n, MLA_V), lambda i, h, j: (i, h, 0, 0)))
        args += [k, v]
    n_keys = sum(src_lens)
    return pl.pallas_call(
        functools.partial(_mla_kernel, src_lens=tuple(src_lens), tile=tile),
        grid=(b, hq, tq_all // tq),
        in_specs=in_specs,
        out_specs=pl.BlockSpec((1, tq, MLA_V), lambda i, h, j: (i, j, h)),
        out_shape=jax.ShapeDtypeStruct((b, tq_all, hq * MLA_V), F32),
        scratch_shapes=(
            [pltpu.VMEM((tile, n_keys), F32)] * 2
            + [pltpu.VMEM((tile, n_keys), BF16)] * 2
            + [pltpu.VMEM((tile, 128), F32)] * 2),
        compiler_params=_cparams(("arbitrary", "arbitrary", "arbitrary")),
        name="mla_%d" % len(kv_sources),
    )(*args)


def _swa_kernel(*refs, s_len, nblk):
    latent = s_len > 0
    if latent:
        q_ref, kl_ref, vl_ref, kc_ref, vc_ref, sink_ref, o_ref = refs
    else:
        q_ref, kc_ref, vc_ref, sink_ref, o_ref = refs
    kc = kc_ref[0]
    vc = vc_ref[0]
    rows = SWA_GROUP * SWA_BLOCK
    win = 3 * SWA_BLOCK
    chains = [(blk, kh) for blk in range(nblk) for kh in range(SWA_KV_HEADS)]
    kcat, vcat, valid = {}, {}, {}
    n_win_tiles = win // 128 if latent else 0
    if latent:
        d = (lax.broadcasted_iota(jnp.int32, (rows, win), 1)
             - (lax.broadcasted_iota(jnp.int32, (rows, win), 0) & (SWA_BLOCK - 1)))
        for blk in range(nblk):
            n = pl.program_id(1) * nblk + blk
            start = jnp.clip((n - 1) * SWA_BLOCK, 0, s_len - win)
            start = pl.multiple_of(start, SWA_BLOCK)
            kcat[blk] = jnp.concatenate([kl_ref[0, pl.ds(start, win), :], kc], axis=0)
            vcat[blk] = jnp.concatenate([vl_ref[0, pl.ds(start, win), :], vc], axis=0)
            valid[blk] = jnp.abs(d + (start - n * SWA_BLOCK)) <= SWA_WINDOW
    else:
        for blk in range(nblk):
            kcat[blk], vcat[blk] = kc, vc

    s, snk, mx = {}, {}, {}
    for c in chains:
        blk, kh = c
        heads = range(kh * SWA_GROUP, (kh + 1) * SWA_GROUP)
        lo, hi = kh * SWA_DH, (kh + 1) * SWA_DH
        q = q_ref[0, blk * SWA_BLOCK:(blk + 1) * SWA_BLOCK, :]
        qs = jnp.concatenate([q[:, h * SWA_DH:(h + 1) * SWA_DH] for h in heads], axis=0)
        snk[c] = jnp.concatenate(
            [jnp.broadcast_to(sink_ref[h:h + 1, 0:1] * LOG2E, (SWA_BLOCK, 1)) for h in heads], axis=0)
        sc = _dot_nt(qs, kcat[blk][:, lo:hi])
        tiles = [sc[:, j * 128:(j + 1) * 128] for j in range(sc.shape[1] // 128)]
        for j in range(n_win_tiles):
            tiles[j] = jnp.where(valid[blk][:, j * 128:(j + 1) * 128], tiles[j], NEG_INF)
        s[c] = tiles
    for c in chains:
        mm = functools.reduce(jnp.maximum, s[c])
        mx[c] = jnp.maximum(jnp.max(mm, axis=1, keepdims=True), snk[c])
    prob, den = {}, {}
    for c in chains:
        ps = [jnp.exp2(t - mx[c]) for t in s[c]]
        den[c] = jnp.sum(functools.reduce(jnp.add, ps), axis=1, keepdims=True) + jnp.exp2(snk[c] - mx[c])
        prob[c] = jnp.concatenate([p.astype(BF16) for p in ps], axis=1)
    for c in chains:
        blk, kh = c
        lo, hi = kh * SWA_DH, (kh + 1) * SWA_DH
        o = _dot(prob[c], vcat[blk][:, lo:hi]) / den[c]
        for g in range(SWA_GROUP):
            h = kh * SWA_GROUP + g
            o_ref[0, blk * SWA_BLOCK:(blk + 1) * SWA_BLOCK, h * SWA_DH:(h + 1) * SWA_DH] = (
                o[g * SWA_BLOCK:(g + 1) * SWA_BLOCK])


def _swa(q, k_lat, v_lat, k_ctx, v_ctx, params, l):
    sink_b = params["sink"]
    b, tq_all, dq = q.shape
    dkv = k_ctx.shape[2]
    n_ctx = k_ctx.shape[1]
    latent = k_lat is not None
    nblk = min(SWA_STEP_BLOCKS, tq_all // SWA_BLOCK)
    tq = nblk * SWA_BLOCK
    in_specs = [pl.BlockSpec((1, tq, dq), lambda i, j: (i, j, 0))]
    args = [q]
    if latent:
        s_len = k_lat.shape[1]
        in_specs += [pl.BlockSpec((1, s_len, dkv), lambda i, j: (i, 0, 0))] * 2
        args += [k_lat, v_lat]
    else:
        s_len = 0
    in_specs += [pl.BlockSpec((1, n_ctx, dkv), lambda i, j: (i, 0, 0))] * 2
    in_specs += [_layer_spec(sink_b, l)]
    args += [k_ctx, v_ctx, sink_b]
    return pl.pallas_call(
        functools.partial(_swa_kernel, s_len=s_len, nblk=nblk),
        grid=(b, tq_all // tq),
        in_specs=in_specs,
        out_specs=pl.BlockSpec((1, tq, dq), lambda i, j: (i, j, 0)),
        out_shape=jax.ShapeDtypeStruct((b, tq_all, dq), F32),
        compiler_params=_cparams(("arbitrary", "arbitrary")),
        name="swa_lat" if latent else "swa_ctx",
    )(*args)


CONV_ROWS = 32


def _conv_rows(prev_ref, cur_ref, next_ref, w_ref, b_ref, lg_ref, lb_ref, ext_ref, sh_ref, nt):
    j = pl.program_id(1)
    tm = cur_ref.shape[1]
    hl = CONV_HALO
    ext_ref[0:hl] = jnp.where(j > 0, prev_ref[0, tm - hl:tm, :], 0.0)
    ext_ref[hl:hl + tm] = cur_ref[0]
    ext_ref[hl + tm:2 * hl + tm] = jnp.where(j < nt - 1, next_ref[0, 0:hl, :], 0.0)
    n_sh = sh_ref.shape[1]
    for sb in range(8):
        sh_ref[sb] = ext_ref[sb:sb + n_sh, :]
    off = hl - CONV_K // 2
    for r0 in range(0, tm, CONV_ROWS):
        acc = None
        for k in range(CONV_K):
            sb, a8 = (off + k) % 8, (off + k) // 8 * 8
            term = sh_ref[sb, r0 + a8:r0 + a8 + CONV_ROWS, :] * w_ref[k:k + 1, :]
            acc = term if acc is None else acc + term
        hcv = acc + b_ref[...]
        mu = jnp.mean(hcv, axis=-1, keepdims=True)
        xc = hcv - mu
        y = xc * lax.rsqrt(jnp.mean(xc * xc, axis=-1, keepdims=True) + EPS) * lg_ref[...] + lb_ref[...]
        yield r0, y * jax.nn.sigmoid(y)


def _merge_kernel(oa_ref, ob_ref, gp_ref, gc_ref, gn_ref, cw_ref, cb_ref, lg_ref, lb_ref,
                  on_ref, wout_ref, x_ref, mod_ref, gm_ref, x1_ref, h2_ref, ext_ref, sh_ref, yc_ref, *, nt):
    on = on_ref[...]
    na = oa_ref.shape[2]
    nb = ob_ref.shape[2]
    yab = jnp.concatenate([_rms(oa_ref[0], on[:, 0:na]), _rms(ob_ref[0], on[:, na:na + nb])],
                          axis=1).astype(BF16)
    proj = _dot(yab, wout_ref[0:na + nb, :])
    for r0, oc in _conv_rows(gp_ref, gc_ref, gn_ref, cw_ref, cb_ref, lg_ref, lb_ref, ext_ref, sh_ref, nt):
        yc_ref[r0:r0 + CONV_ROWS, :] = _rms(oc, on[:, na + nb:]).astype(BF16)
    proj = proj + _dot(yc_ref[...], wout_ref[na + nb:, :])
    m = mod_ref[...]
    x1 = x_ref[0] + m[2:3] * proj
    x1_ref[0] = x1
    h2_ref[0] = (_rms(x1, gm_ref[...]) * (1.0 + m[4:5]) + m[3:4]).astype(BF16)


def _merge(oa, ob, glu, x, mod, mod_row, params, l):
    b, t, d = x.shape
    ch = glu.shape[2]
    tm = ROW_TILE
    nt = t // tm
    row = lambda i, j: (i, j, 0)
    stacks = [params[k] for k in ("conv_w", "conv_b", "conv_ln_g", "conv_ln_b", "out_norm", "w_out")]
    g_mlp = params["norm_mlp"]
    return pl.pallas_call(
        functools.partial(_merge_kernel, nt=nt),
        grid=(b, nt),
        in_specs=[
            pl.BlockSpec((1, tm, oa.shape[2]), row),
            pl.BlockSpec((1, tm, ob.shape[2]), row),
            pl.BlockSpec((1, tm, ch), lambda i, j: (i, jnp.maximum(j - 1, 0), 0)),
            pl.BlockSpec((1, tm, ch), row),
            pl.BlockSpec((1, tm, ch), lambda i, j: (i, jnp.minimum(j + 1, nt - 1), 0)),
        ] + [_layer_spec(a, l) for a in stacks] + [
            pl.BlockSpec((1, tm, d), row),
            _mod_spec(mod, l, mod_row),
            _layer_spec(g_mlp, l),
        ],
        out_specs=(pl.BlockSpec((1, tm, d), row), pl.BlockSpec((1, tm, d), row)),
        out_shape=(jax.ShapeDtypeStruct((b, t, d), F32), jax.ShapeDtypeStruct((b, t, d), BF16)),
        scratch_shapes=[pltpu.VMEM((tm + 2 * CONV_HALO, ch), F32),
                        pltpu.VMEM((8, tm + 2 * CONV_HALO - 8, ch), F32),
                        pltpu.VMEM((tm, ch), BF16)],
        compiler_params=_cparams(("arbitrary", "arbitrary")),
        name="merge",
    )(oa, ob, glu, glu, glu, *stacks, x, mod, g_mlp)


def _mlp_kernel(*refs, nf, final):
    if final:
        h_ref, w1_ref, w2_ref, x_ref, mod_ref, fn_ref, o_ref, acc_ref = refs
    else:
        h_ref, w1_ref, w2_ref, x_ref, mod_ref, o_ref, acc_ref = refs
    j = pl.program_id(2)

    @pl.when(j == 0)
    def _():
        acc_ref[...] = jnp.zeros_like(acc_ref)

    a = jnp.square(jnp.maximum(_dot(h_ref[0], w1_ref[...]), 0.0))
    acc_ref[...] += _dot(a.astype(BF16), w2_ref[...])

    @pl.when(j == nf - 1)
    def _():
        out = x_ref[0] + mod_ref[5:6, :] * acc_ref[...]
        if final:
            out = _rms(out, fn_ref[...])
        o_ref[0] = out


def _mlp(h2, x1, mod, mod_row, params, l, final_norm=None):
    b, t, d = x1.shape
    w1, w2 = params["w1"], params["w2"]
    dff = w1.shape[2]
    tr = min(MLP_ROWS, t)
    tf = MLP_FF_TILE
    nf = dff // tf
    final = final_norm is not None
    row = lambda i, r, j: (i, r, 0)
    in_specs = [
        pl.BlockSpec((1, tr, d), row),
        pl.BlockSpec((None, d, tf), lambda i, r, j: (l, 0, j)),
        pl.BlockSpec((None, tf, d), lambda i, r, j: (l, j, 0)),
        pl.BlockSpec((1, tr, d), row),
        _mod_spec(mod, l, mod_row),
    ]
    args = [h2, w1, w2, x1, mod]
    if final:
        in_specs.append(pl.BlockSpec(final_norm.shape, lambda i, r, j: (0, 0)))
        args.append(final_norm)
    return pl.pallas_call(
        functools.partial(_mlp_kernel, nf=nf, final=final),
        grid=(b, t // tr, nf),
        in_specs=in_specs,
        out_specs=pl.BlockSpec((1, tr, d), row),
        out_shape=jax.ShapeDtypeStruct((b, t, d), F32),
        scratch_shapes=[pltpu.VMEM((tr, d), F32)],
        compiler_params=_cparams(("arbitrary", "arbitrary", "arbitrary")),
        name="mlp_final" if final else "mlp",
    )(*args)


def _rope_tables(n_tok):
    rows = n_tok // GRID_W
    row = jnp.repeat(jnp.arange(rows, dtype=F32), GRID_W)
    col = jnp.tile(jnp.arange(GRID_W, dtype=F32), rows)
    n_freq = MLA_ROPE // 4
    inv_freq = ROPE_THETA ** (-jnp.arange(n_freq, dtype=F32) / n_freq)
    ar = row[:, None] * inv_freq
    ac = col[:, None] * inv_freq
    cos = jnp.concatenate([jnp.cos(ar), jnp.cos(ar), jnp.cos(ac), jnp.cos(ac)], axis=1)
    sin = jnp.concatenate([-jnp.sin(ar), jnp.sin(ar), -jnp.sin(ac), jnp.sin(ac)], axis=1)
    return jnp.tile(cos, (1, 2)), jnp.tile(sin, (1, 2))


def kernel(x, c, ctx, c_ctx, ada_w, ada_b, norm_mix, norm_mlp, w_in, mla_q_norm, mla_w_uq, mla_kv_norm, mla_w_ukv, swa_sink, conv_w, conv_b, conv_ln_g, conv_ln_b, out_norm, w_out, mlp_w1, mlp_w2, final_norm):
    b, s, d = x.shape
    n_ctx = ctx.shape[1]
    depth = ada_w.shape[0]
    assert s % (2 * MLA_Q_TILE) == 0 and s % MLP_ROWS == 0 and n_ctx % ROW_TILE == 0
    assert s % (SWA_STEP_BLOCKS * SWA_BLOCK) == 0
    assert s >= 3 * SWA_BLOCK and b + 1 <= 8

    cvec = jnp.concatenate([c, c_ctx[None, :], jnp.zeros((8 - b - 1, d), F32)], axis=0)
    mod = _ada(cvec, ada_w, ada_b).reshape(depth, 8, 6, d)
    rope_tabs = _rope_tables(s)

    vec = lambda v: v.reshape(depth, 1, -1)
    params = {
        "norm_mix": vec(norm_mix),
        "w_in": jnp.pad(w_in.astype(BF16), ((0, 0), (0, 0), (0, IN_PAD_WIDTH - w_in.shape[2]))),
        "q_norm": vec(mla_q_norm),
        "w_uq": jnp.pad(mla_w_uq.astype(BF16), ((0, 0), (0, 0), (0, 0), (0, MLA_QK_PAD - MLA_NOPE - MLA_ROPE))
                        ).reshape(depth, MLA_RANK, MLA_HEADS * MLA_QK_PAD),
        "kv_norm": vec(mla_kv_norm),
        "w_ukv": mla_w_ukv.astype(BF16).reshape(depth, MLA_RANK, MLA_HEADS * (MLA_NOPE + MLA_V)),
        "sink": jnp.broadcast_to(swa_sink[:, :, None], (depth, SWA_HEADS, 128)),
        "conv_w": conv_w.reshape(depth, CONV_K, CONV_CH),
        "conv_b": vec(conv_b),
        "conv_ln_g": vec(conv_ln_g),
        "conv_ln_b": vec(conv_ln_b),
        "out_norm": vec(out_norm),
        "w_out": w_out.astype(BF16),
        "norm_mlp": vec(norm_mlp),
        "w1": mlp_w1.astype(BF16),
        "w2": mlp_w2.astype(BF16),
    }

    xc = ctx
    for l in range(depth):
        update_ctx = l < depth - 1
        qa, ka, va, bq, bk, bv, glu = _premix(x, mod, None, params, l, rope_tabs)
        qa_c, ka_c, va_c, bq_c, bk_c, bv_c, glu_c = _premix(xc, mod, b, params, l, None)

        oa = _mla(qa, [(ka, va), (ka_c, va_c)])
        ob = _swa(bq, bk, bv, bk_c, bv_c, params, l)
        x1, h2 = _merge(oa, ob, glu, x, mod, None, params, l)
        x = _mlp(h2, x1, mod, None, params, l, None if update_ctx else final_norm.reshape(1, d))

        if update_ctx:
            oa_c = _mla(qa_c, [(ka_c, va_c)])
            ob_c = _swa(bq_c, None, None, bk_c, bv_c, params, l)
            xc1, h2c = _merge(oa_c, ob_c, glu_c, xc, mod, b, params, l)
            flat = lambda a: a.reshape(1, b * n_ctx, d)
            xc = _mlp(flat(h2c), flat(xc1), mod, b, params, l).reshape(b, n_ctx, d)
    return x
```

```python
import functools

import jax
import jax.numpy as jnp
from jax import lax
from jax.experimental import pallas as pl
from jax.experimental.pallas import tpu as pltpu

F32 = jnp.float32
BF16 = jnp.bfloat16

EPS = 1e-6
NEG_INF = -1e30
GRID_W = 64
ROPE_THETA = 10000.0

MLA_HEADS = 8
MLA_RANK = 512
MLA_NOPE = 128
MLA_ROPE = 64
MLA_V = 128
MLA_QK_PAD = 256
LOG2E = 1.4426950408889634
MLA_Q_SCALE = (MLA_NOPE + MLA_ROPE) ** -0.5 * LOG2E
SWA_HEADS = 8
SWA_KV_HEADS = 2
SWA_GROUP = SWA_HEADS // SWA_KV_HEADS
SWA_DH = 64
SWA_WINDOW = 128
SWA_BLOCK = 128
SWA_Q_SCALE = SWA_DH ** -0.5 * LOG2E
CONV_CH = 512
CONV_K = 31
CONV_HALO = 16

OFF_AQ = 0
OFF_AKV = 512
OFF_KR = 1024
IN_PAD_WIDTH = 2944
TAIL_BQ = 0
TAIL_BK = 512
TAIL_BV = 640
TAIL_C = 768

ROW_TILE = 256
PREMIX_ROWS = 512
MLP_ROWS = 512
MLP_FF_TILE = 1024
MLA_Q_TILE = 256
MLA_STEP_ROWS = 2048
MLA_SM_ROWS = 16
SWA_STEP_BLOCKS = 4
ADA_N_TILE = 1024
VMEM_LIMIT = 56 * 1024 * 1024


def _cparams(sem):
    return pltpu.CompilerParams(dimension_semantics=sem, vmem_limit_bytes=VMEM_LIMIT)


def _layer_spec(stack, l):
    _, a, b = stack.shape
    return pl.BlockSpec((None, a, b), lambda *_: (l, 0, 0), pipeline_mode=pl.Buffered(1))


def _mod_spec(mod, l, row):
    d = mod.shape[-1]
    return pl.BlockSpec((None, None, 6, d), lambda i, *_: (l, i if row is None else row, 0, 0))


def _rms(x, g):
    return x * lax.rsqrt(jnp.mean(x * x, axis=-1, keepdims=True) + EPS) * g


def _dot(a, b):
    return jnp.dot(a, b, preferred_element_type=F32)


def _dot_nt(a, b):
    return lax.dot_general(a, b, (((1,), (1,)), ((), ())), preferred_element_type=F32)


def _ada_kernel(c_ref, w_ref, b_ref, o_ref):
    c = c_ref[...]
    s = c * jax.nn.sigmoid(c)
    o_ref[0] = _dot(s.astype(BF16), w_ref[0].astype(BF16)) + b_ref[0]


def _ada(cvec, ada_w, ada_b):
    n_layers, d, n = ada_w.shape
    rows = cvec.shape[0]
    tn = ADA_N_TILE
    return pl.pallas_call(
        _ada_kernel,
        grid=(n_layers, n // tn),
        in_specs=[
            pl.BlockSpec((rows, d), lambda l, j: (0, 0)),
            pl.BlockSpec((1, d, tn), lambda l, j: (l, 0, j)),
            pl.BlockSpec((1, 1, tn), lambda l, j: (l, 0, j)),
        ],
        out_specs=pl.BlockSpec((1, rows, tn), lambda l, j: (l, 0, j)),
        out_shape=jax.ShapeDtypeStruct((n_layers, rows, n), F32),
        compiler_params=_cparams(("arbitrary", "arbitrary")),
        name="ada",
    )(cvec, ada_w, ada_b.reshape(n_layers, 1, n))


def _rope128(t, cos, sin):
    lane = lax.broadcasted_iota(jnp.int32, t.shape, 1)
    up = pltpu.roll(t, 128 - 16, 1)
    dn = pltpu.roll(t, 16, 1)
    sw = jnp.where((lane & 16) == 0, up, dn)
    return t * cos + sw * sin


def _premix_kernel(*refs, use_rope):
    if use_rope:
        (x_ref, mod_ref, g_ref, win_ref, qn_ref, wuq_ref, kvn_ref, wukv_ref, cos_ref, sin_ref,
         qa_ref, ka_ref, va_ref, bq_ref, bk_ref, bv_ref, glu_ref) = refs
        cos = cos_ref[...]
        sin = sin_ref[...]
        rope = lambda t: _rope128(t, cos, sin)
    else:
        (x_ref, mod_ref, g_ref, win_ref, qn_ref, wuq_ref, kvn_ref, wukv_ref,
         qa_ref, ka_ref, va_ref, bq_ref, bk_ref, bv_ref, glu_ref) = refs
        rope = lambda t: t

    m = mod_ref[...]
    h = _rms(x_ref[0], g_ref[...]) * (1.0 + m[1:2]) + m[0:1]
    p = _dot(h.astype(BF16), win_ref[...])
    tail = pltpu.roll(p[:, OFF_KR:], IN_PAD_WIDTH - OFF_KR - MLA_ROPE, 1)
    lane = lax.broadcasted_iota(jnp.int32, (p.shape[0], 128), 1)
    kr = jnp.where(lane < MLA_ROPE, p[:, OFF_KR:OFF_KR + 128], 0.0)

    qn = _rms(p[:, OFF_AQ:OFF_AQ + MLA_RANK], qn_ref[...])
    q = _dot(qn.astype(BF16), wuq_ref[...])
    for hd in range(MLA_HEADS):
        c0 = hd * MLA_QK_PAD
        qa_ref[0, hd, :, 0:128] = (q[:, c0:c0 + 128] * MLA_Q_SCALE).astype(BF16)
        qa_ref[0, hd, :, 128:256] = (rope(q[:, c0 + 128:c0 + 256]) * MLA_Q_SCALE).astype(BF16)

    kvn = _rms(p[:, OFF_AKV:OFF_AKV + MLA_RANK], kvn_ref[...])
    kv = _dot(kvn.astype(BF16), wukv_ref[...])
    kpe = rope(kr).astype(BF16)
    for hd in range(MLA_HEADS):
        c0 = hd * (MLA_NOPE + MLA_V)
        ka_ref[0, hd, :, 0:128] = kv[:, c0:c0 + 128].astype(BF16)
        ka_ref[0, hd, :, 128:256] = kpe
        va_ref[0, hd] = kv[:, c0 + 128:c0 + 256].astype(BF16)

    for t in range(SWA_HEADS * SWA_DH // 128):
        c0 = TAIL_BQ + t * 128
        bq_ref[0, :, t * 128:(t + 1) * 128] = (rope(tail[:, c0:c0 + 128]) * SWA_Q_SCALE).astype(BF16)
    bk_ref[0] = rope(tail[:, TAIL_BK:TAIL_BK + 128]).astype(BF16)
    bv_ref[0] = tail[:, TAIL_BV:TAIL_BV + 128].astype(BF16)

    glu_ref[0] = (tail[:, TAIL_C:TAIL_C + CONV_CH]
                  * jax.nn.sigmoid(tail[:, TAIL_C + CONV_CH:TAIL_C + 2 * CONV_CH]))


def _premix(x, mod, mod_row, params, l, rope_tabs):
    b, t, d = x.shape
    tm = min(PREMIX_ROWS, t)
    use_rope = rope_tabs is not None
    stacks = [params[k] for k in ("norm_mix", "w_in", "q_norm", "w_uq", "kv_norm", "w_ukv")]
    in_specs = [pl.BlockSpec((1, tm, d), lambda i, j: (i, j, 0)), _mod_spec(mod, l, mod_row)]
    in_specs += [_layer_spec(a, l) for a in stacks]
    args = [x, mod] + stacks
    if use_rope:
        in_specs += [pl.BlockSpec((tm, 128), lambda i, j: (j, 0))] * 2
        args += list(rope_tabs)
    hq = MLA_HEADS
    out_shape = (
        jax.ShapeDtypeStruct((b, hq, t, MLA_QK_PAD), BF16),
        jax.ShapeDtypeStruct((b, hq, t, MLA_QK_PAD), BF16),
        jax.ShapeDtypeStruct((b, hq, t, MLA_V), BF16),
        jax.ShapeDtypeStruct((b, t, SWA_HEADS * SWA_DH), BF16),
        jax.ShapeDtypeStruct((b, t, SWA_KV_HEADS * SWA_DH), BF16),
        jax.ShapeDtypeStruct((b, t, SWA_KV_HEADS * SWA_DH), BF16),
        jax.ShapeDtypeStruct((b, t, CONV_CH), F32),
    )
    out_specs = (
        pl.BlockSpec((1, hq, tm, MLA_QK_PAD), lambda i, j: (i, 0, j, 0)),
        pl.BlockSpec((1, hq, tm, MLA_QK_PAD), lambda i, j: (i, 0, j, 0)),
        pl.BlockSpec((1, hq, tm, MLA_V), lambda i, j: (i, 0, j, 0)),
        pl.BlockSpec((1, tm, SWA_HEADS * SWA_DH), lambda i, j: (i, j, 0)),
        pl.BlockSpec((1, tm, SWA_KV_HEADS * SWA_DH), lambda i, j: (i, j, 0)),
        pl.BlockSpec((1, tm, SWA_KV_HEADS * SWA_DH), lambda i, j: (i, j, 0)),
        pl.BlockSpec((1, tm, CONV_CH), lambda i, j: (i, j, 0)),
    )
    return pl.pallas_call(
        functools.partial(_premix_kernel, use_rope=use_rope),
        grid=(b, t // tm),
        in_specs=in_specs,
        out_specs=out_specs,
        out_shape=out_shape,
        compiler_params=_cparams(("arbitrary", "arbitrary")),
        name="premix_rope" if use_rope else "premix",
    )(*args)


def _mla_kernel(*refs, src_lens, tile):
    n_src = len(src_lens)
    q_ref = refs[0]
    kv_refs = refs[1:1 + 2 * n_src]
    o_ref = refs[1 + 2 * n_src]
    s_refs = refs[2 + 2 * n_src:4 + 2 * n_src]
    p_refs = refs[4 + 2 * n_src:6 + 2 * n_src]
    l_refs = refs[6 + 2 * n_src:8 + 2 * n_src]
    offs = [sum(src_lens[:i]) for i in range(n_src)]
    n_pairs = q_ref.shape[2] // (2 * tile)

    def pair(pr):
        def rows(t):
            start = (2 * pr + t) * tile
            return pl.ds(start if isinstance(pr, int) else pl.multiple_of(start, tile), tile)

        _mla_pair(q_ref, kv_refs, o_ref, s_refs, p_refs, l_refs, rows, src_lens, offs, tile)

    if n_pairs == 1:
        pair(0)
    else:
        pl.loop(0, n_pairs)(pair)


def _mla_pair(q_ref, kv_refs, o_ref, s_refs, p_refs, l_refs, rows, src_lens, offs, tile):
    n_src = len(src_lens)

    def scores(t):
        qt = q_ref[0, 0, rows(t), :]
        for si in range(n_src):
            s_refs[t][:, offs[si]:offs[si] + src_lens[si]] = _dot_nt(qt, kv_refs[2 * si][0, 0])

    def softmax(t):
        for r in range(0, tile, MLA_SM_ROWS):
            rs = slice(r, r + MLA_SM_ROWS)
            sb = s_refs[t][rs, :]
            p = jnp.exp2(sb - jnp.max(sb, axis=1, keepdims=True))
            l_refs[t][rs, :] = jnp.broadcast_to(jnp.sum(p, axis=1, keepdims=True), (MLA_SM_ROWS, 128))
            p_refs[t][rs, :] = p.astype(BF16)

    def values(t):
        acc = None
        for si in range(n_src):
            part = _dot(p_refs[t][:, offs[si]:offs[si] + src_lens[si]], kv_refs[2 * si + 1][0, 0])
            acc = part if acc is None else acc + part
        o_ref[0, rows(t), :] = acc / l_refs[t][...]

    scores(0)
    scores(1)
    softmax(0)
    values(0)
    softmax(1)
    values(1)


def _mla(q, kv_sources):
    b, hq, tq_all, dq = q.shape
    tile = min(MLA_Q_TILE, tq_all // 2)
    tq = min(MLA_STEP_ROWS, tq_all)
    assert tq_all % tq == 0 and tq % (2 * tile) == 0
    in_specs = [pl.BlockSpec((1, 1, tq, dq), lambda i, h, j: (i, h, j, 0))]
    args = [q]
    src_lens = []
    for k, v in kv_sources:
        n = k.shape[2]
        src_lens.append(n)
        in_specs.append(pl.BlockSpec((1, 1, n, dq), lambda i, h, j: (i, h, 0, 0)))
        in_specs.append(pl.BlockSpec((1, 1, n, MLA_V), lambda i, h, j: (i, h, 0, 0)))
        args += [k, v]
    n_keys = sum(src_lens)
    return pl.pallas_call(
        functools.partial(_mla_kernel, src_lens=tuple(src_lens), tile=tile),
        grid=(b, hq, tq_all // tq),
        in_specs=in_specs,
        out_specs=pl.BlockSpec((1, tq, MLA_V), lambda i, h, j: (i, j, h)),
        out_shape=jax.ShapeDtypeStruct((b, tq_all, hq * MLA_V), F32),
        scratch_shapes=(
            [pltpu.VMEM((tile, n_keys), F32)] * 2
            + [pltpu.VMEM((tile, n_keys), BF16)] * 2
            + [pltpu.VMEM((tile, 128), F32)] * 2),
        compiler_params=_cparams(("arbitrary", "arbitrary", "arbitrary")),
        name="mla_%d" % len(kv_sources),
    )(*args)


def _swa_kernel(*refs, s_len, nblk):
    latent = s_len > 0
    if latent:
        q_ref, kl_ref, vl_ref, kc_ref, vc_ref, sink_ref, o_ref = refs
    else:
        q_ref, kc_ref, vc_ref, sink_ref, o_ref = refs
    kc = kc_ref[0]
    vc = vc_ref[0]
    rows = SWA_GROUP * SWA_BLOCK
    win = 3 * SWA_BLOCK
    chains = [(blk, kh) for blk in range(nblk) for kh in range(SWA_KV_HEADS)]
    kcat, vcat, valid = {}, {}, {}
    n_win_tiles = win // 128 if latent else 0
    if latent:
        d = (lax.broadcasted_iota(jnp.int32, (rows, win), 1)
             - (lax.broadcasted_iota(jnp.int32, (rows, win), 0) & (SWA_BLOCK - 1)))
        for blk in range(nblk):
            n = pl.program_id(1) * nblk + blk
            start = jnp.clip((n - 1) * SWA_BLOCK, 0, s_len - win)
            start = pl.multiple_of(start, SWA_BLOCK)
            kcat[blk] = jnp.concatenate([kl_ref[0, pl.ds(start, win), :], kc], axis=0)
            vcat[blk] = jnp.concatenate([vl_ref[0, pl.ds(start, win), :], vc], axis=0)
            valid[blk] = jnp.abs(d + (start - n * SWA_BLOCK)) <= SWA_WINDOW
    else:
        for blk in range(nblk):
            kcat[blk], vcat[blk] = kc, vc

    s, snk, mx = {}, {}, {}
    for c in chains:
        blk, kh = c
        heads = range(kh * SWA_GROUP, (kh + 1) * SWA_GROUP)
        lo, hi = kh * SWA_DH, (kh + 1) * SWA_DH
        q = q_ref[0, blk * SWA_BLOCK:(blk + 1) * SWA_BLOCK, :]
        qs = jnp.concatenate([q[:, h * SWA_DH:(h + 1) * SWA_DH] for h in heads], axis=0)
        snk[c] = jnp.concatenate(
            [jnp.broadcast_to(sink_ref[h:h + 1, 0:1] * LOG2E, (SWA_BLOCK, 1)) for h in heads], axis=0)
        sc = _dot_nt(qs, kcat[blk][:, lo:hi])
        tiles = [sc[:, j * 128:(j + 1) * 128] for j in range(sc.shape[1] // 128)]
        for j in range(n_win_tiles):
            tiles[j] = jnp.where(valid[blk][:, j * 128:(j + 1) * 128], tiles[j], NEG_INF)
        s[c] = tiles
    for c in chains:
        mm = functools.reduce(jnp.maximum, s[c])
        mx[c] = jnp.maximum(jnp.max(mm, axis=1, keepdims=True), snk[c])
    prob, den = {}, {}
    for c in chains:
        ps = [jnp.exp2(t - mx[c]) for t in s[c]]
        den[c] = jnp.sum(functools.reduce(jnp.add, ps), axis=1, keepdims=True) + jnp.exp2(snk[c] - mx[c])
        prob[c] = jnp.concatenate([p.astype(BF16) for p in ps], axis=1)
    for c in chains:
        blk, kh = c
        lo, hi = kh * SWA_DH, (kh + 1) * SWA_DH
        o = _dot(prob[c], vcat[blk][:, lo:hi]) / den[c]
        for g in range(SWA_GROUP):
            h = kh * SWA_GROUP + g
            o_ref[0, blk * SWA_BLOCK:(blk + 1) * SWA_BLOCK, h * SWA_DH:(h + 1) * SWA_DH] = (
                o[g * SWA_BLOCK:(g + 1) * SWA_BLOCK])


def _swa(q, k_lat, v_lat, k_ctx, v_ctx, params, l):
    sink_b = params["sink"]
    b, tq_all, dq = q.shape
    dkv = k_ctx.shape[2]
    n_ctx = k_ctx.shape[1]
    latent = k_lat is not None
    nblk = min(SWA_STEP_BLOCKS, tq_all // SWA_BLOCK)
    tq = nblk * SWA_BLOCK
    in_specs = [pl.BlockSpec((1, tq, dq), lambda i, j: (i, j, 0))]
    args = [q]
    if latent:
        s_len = k_lat.shape[1]
        in_specs += [pl.BlockSpec((1, s_len, dkv), lambda i, j: (i, 0, 0))] * 2
        args += [k_lat, v_lat]
    else:
        s_len = 0
    in_specs += [pl.BlockSpec((1, n_ctx, dkv), lambda i, j: (i, 0, 0))] * 2
    in_specs += [_layer_spec(sink_b, l)]
    args += [k_ctx, v_ctx, sink_b]
    return pl.pallas_call(
        functools.partial(_swa_kernel, s_len=s_len, nblk=nblk),
        grid=(b, tq_all // tq),
        in_specs=in_specs,
        out_specs=pl.BlockSpec((1, tq, dq), lambda i, j: (i, j, 0)),
        out_shape=jax.ShapeDtypeStruct((b, tq_all, dq), F32),
        compiler_params=_cparams(("arbitrary", "arbitrary")),
        name="swa_lat" if latent else "swa_ctx",
    )(*args)


CONV_ROWS = 64


def _conv_rows(prev_ref, cur_ref, next_ref, w_ref, b_ref, lg_ref, lb_ref, ext_ref, sh_ref, nt):
    j = pl.program_id(1)
    tm = cur_ref.shape[1]
    hl = CONV_HALO
    ext_ref[0:hl] = jnp.where(j > 0, prev_ref[0, tm - hl:tm, :], 0.0)
    ext_ref[hl:hl + tm] = cur_ref[0]
    ext_ref[hl + tm:2 * hl + tm] = jnp.where(j < nt - 1, next_ref[0, 0:hl, :], 0.0)
    n_sh = sh_ref.shape[1]
    for sb in range(8):
        sh_ref[sb] = ext_ref[sb:sb + n_sh, :]
    off = hl - CONV_K // 2
    for r0 in range(0, tm, CONV_ROWS):
        acc = None
        for k in range(CONV_K):
            sb, a8 = (off + k) % 8, (off + k) // 8 * 8
            tap = sh_ref[sb, r0 + a8:r0 + a8 + CONV_ROWS, :].reshape(CONV_ROWS // 8, 8, -1)
            term = (tap * w_ref[k * 8:(k + 1) * 8, :][None]).reshape(CONV_ROWS, -1)
            acc = term if acc is None else acc + term
        hcv = acc + b_ref[...]
        mu = jnp.mean(hcv, axis=-1, keepdims=True)
        xc = hcv - mu
        y = xc * lax.rsqrt(jnp.mean(xc * xc, axis=-1, keepdims=True) + EPS) * lg_ref[...] + lb_ref[...]
        yield r0, y * jax.nn.sigmoid(y)


def _merge_kernel(oa_ref, ob_ref, gp_ref, gc_ref, gn_ref, cw_ref, cb_ref, lg_ref, lb_ref,
                  on_ref, wout_ref, x_ref, mod_ref, gm_ref, x1_ref, h2_ref, ext_ref, sh_ref, yc_ref, *, nt):
    on = on_ref[...]
    na = oa_ref.shape[2]
    nb = ob_ref.shape[2]
    yab = jnp.concatenate([_rms(oa_ref[0], on[:, 0:na]), _rms(ob_ref[0], on[:, na:na + nb])],
                          axis=1).astype(BF16)
    proj = _dot(yab, wout_ref[0:na + nb, :])
    for r0, oc in _conv_rows(gp_ref, gc_ref, gn_ref, cw_ref, cb_ref, lg_ref, lb_ref, ext_ref, sh_ref, nt):
        yc_ref[r0:r0 + CONV_ROWS, :] = _rms(oc, on[:, na + nb:]).astype(BF16)
    proj = proj + _dot(yc_ref[...], wout_ref[na + nb:, :])
    m = mod_ref[...]
    x1 = x_ref[0] + m[2:3] * proj
    x1_ref[0] = x1
    h2_ref[0] = (_rms(x1, gm_ref[...]) * (1.0 + m[4:5]) + m[3:4]).astype(BF16)


def _merge(oa, ob, glu, x, mod, mod_row, params, l):
    b, t, d = x.shape
    ch = glu.shape[2]
    tm = ROW_TILE
    nt = t // tm
    row = lambda i, j: (i, j, 0)
    stacks = [params[k] for k in ("conv_w", "conv_b", "conv_ln_g", "conv_ln_b", "out_norm", "w_out")]
    g_mlp = params["norm_mlp"]
    return pl.pallas_call(
        functools.partial(_merge_kernel, nt=nt),
        grid=(b, nt),
        in_specs=[
            pl.BlockSpec((1, tm, oa.shape[2]), row),
            pl.BlockSpec((1, tm, ob.shape[2]), row),
            pl.BlockSpec((1, tm, ch), lambda i, j: (i, jnp.maximum(j - 1, 0), 0)),
            pl.BlockSpec((1, tm, ch), row),
            pl.BlockSpec((1, tm, ch), lambda i, j: (i, jnp.minimum(j + 1, nt - 1), 0)),
        ] + [_layer_spec(a, l) for a in stacks] + [
            pl.BlockSpec((1, tm, d), row),
            _mod_spec(mod, l, mod_row),
            _layer_spec(g_mlp, l),
        ],
        out_specs=(pl.BlockSpec((1, tm, d), row), pl.BlockSpec((1, tm, d), row)),
        out_shape=(jax.ShapeDtypeStruct((b, t, d), F32), jax.ShapeDtypeStruct((b, t, d), BF16)),
        scratch_shapes=[pltpu.VMEM((tm + 2 * CONV_HALO, ch), F32),
                        pltpu.VMEM((8, tm + 2 * CONV_HALO - 8, ch), F32),
                        pltpu.VMEM((tm, ch), BF16)],
        compiler_params=_cparams(("arbitrary", "arbitrary")),
        name="merge",
    )(oa, ob, glu, glu, glu, *stacks, x, mod, g_mlp)


def _mlp_kernel(*refs, nf, final):
    if final:
        h_ref, w1_ref, w2_ref, x_ref, mod_ref, fn_ref, o_ref, acc_ref = refs
    else:
        h_ref, w1_ref, w2_ref, x_ref, mod_ref, o_ref, acc_ref = refs
    j = pl.program_id(2)

    @pl.when(j == 0)
    def _():
        acc_ref[...] = jnp.zeros_like(acc_ref)

    a = jnp.square(jnp.maximum(_dot(h_ref[0], w1_ref[...]), 0.0))
    acc_ref[...] += _dot(a.astype(BF16), w2_ref[...])

    @pl.when(j == nf - 1)
    def _():
        out = x_ref[0] + mod_ref[5:6, :] * acc_ref[...]
        if final:
            out = _rms(out, fn_ref[...])
        o_ref[0] = out


def _mlp(h2, x1, mod, mod_row, params, l, final_norm=None):
    b, t, d = x1.shape
    w1, w2 = params["w1"], params["w2"]
    dff = w1.shape[2]
    tr = min(MLP_ROWS, t)
    tf = MLP_FF_TILE
    nf = dff // tf
    final = final_norm is not None
    row = lambda i, r, j: (i, r, 0)
    in_specs = [
        pl.BlockSpec((1, tr, d), row),
        pl.BlockSpec((None, d, tf), lambda i, r, j: (l, 0, j)),
        pl.BlockSpec((None, tf, d), lambda i, r, j: (l, j, 0)),
        pl.BlockSpec((1, tr, d), row),
        _mod_spec(mod, l, mod_row),
    ]
    args = [h2, w1, w2, x1, mod]
    if final:
        in_specs.append(pl.BlockSpec(final_norm.shape, lambda i, r, j: (0, 0)))
        args.append(final_norm)
    return pl.pallas_call(
        functools.partial(_mlp_kernel, nf=nf, final=final),
        grid=(b, t // tr, nf),
        in_specs=in_specs,
        out_specs=pl.BlockSpec((1, tr, d), row),
        out_shape=jax.ShapeDtypeStruct((b, t, d), F32),
        scratch_shapes=[pltpu.VMEM((tr, d), F32)],
        compiler_params=_cparams(("arbitrary", "arbitrary", "arbitrary")),
        name="mlp_final" if final else "mlp",
    )(*args)


def _rope_tables(n_tok):
    rows = n_tok // GRID_W
    row = jnp.repeat(jnp.arange(rows, dtype=F32), GRID_W)
    col = jnp.tile(jnp.arange(GRID_W, dtype=F32), rows)
    n_freq = MLA_ROPE // 4
    inv_freq = ROPE_THETA ** (-jnp.arange(n_freq, dtype=F32) / n_freq)
    ar = row[:, None] * inv_freq
    ac = col[:, None] * inv_freq
    cos = jnp.concatenate([jnp.cos(ar), jnp.cos(ar), jnp.cos(ac), jnp.cos(ac)], axis=1)
    sin = jnp.concatenate([-jnp.sin(ar), jnp.sin(ar), -jnp.sin(ac), jnp.sin(ac)], axis=1)
    return jnp.tile(cos, (1, 2)), jnp.tile(sin, (1, 2))


def kernel(x, c, ctx, c_ctx, ada_w, ada_b, norm_mix, norm_mlp, w_in, mla_q_norm, mla_w_uq, mla_kv_norm, mla_w_ukv, swa_sink, conv_w, conv_b, conv_ln_g, conv_ln_b, out_norm, w_out, mlp_w1, mlp_w2, final_norm):
    b, s, d = x.shape
    n_ctx = ctx.shape[1]
    depth = ada_w.shape[0]
    assert s % (2 * MLA_Q_TILE) == 0 and s % MLP_ROWS == 0 and n_ctx % ROW_TILE == 0
    assert s % (SWA_STEP_BLOCKS * SWA_BLOCK) == 0
    assert s >= 3 * SWA_BLOCK and b + 1 <= 8

    cvec = jnp.concatenate([c, c_ctx[None, :], jnp.zeros((8 - b - 1, d), F32)], axis=0)
    mod = _ada(cvec, ada_w, ada_b).reshape(depth, 8, 6, d)
    rope_tabs = _rope_tables(s)

    vec = lambda v: v.reshape(depth, 1, -1)
    params = {
        "norm_mix": vec(norm_mix),
        "w_in": jnp.pad(w_in.astype(BF16), ((0, 0), (0, 0), (0, IN_PAD_WIDTH - w_in.shape[2]))),
        "q_norm": vec(mla_q_norm),
        "w_uq": jnp.pad(mla_w_uq.astype(BF16), ((0, 0), (0, 0), (0, 0), (0, MLA_QK_PAD - MLA_NOPE - MLA_ROPE))
                        ).reshape(depth, MLA_RANK, MLA_HEADS * MLA_QK_PAD),
        "kv_norm": vec(mla_kv_norm),
        "w_ukv": mla_w_ukv.astype(BF16).reshape(depth, MLA_RANK, MLA_HEADS * (MLA_NOPE + MLA_V)),
        "sink": jnp.broadcast_to(swa_sink[:, :, None], (depth, SWA_HEADS, 128)),
        "conv_w": jnp.broadcast_to(conv_w.reshape(depth, CONV_K, 1, CONV_CH), (depth, CONV_K, 8, CONV_CH)
                                   ).reshape(depth, CONV_K * 8, CONV_CH),
        "conv_b": vec(conv_b),
        "conv_ln_g": vec(conv_ln_g),
        "conv_ln_b": vec(conv_ln_b),
        "out_norm": vec(out_norm),
        "w_out": w_out.astype(BF16),
        "norm_mlp": vec(norm_mlp),
        "w1": mlp_w1.astype(BF16),
        "w2": mlp_w2.astype(BF16),
    }

    xc = ctx
    for l in range(depth):
        update_ctx = l < depth - 1
        qa, ka, va, bq, bk, bv, glu = _premix(x, mod, None, params, l, rope_tabs)
        qa_c, ka_c, va_c, bq_c, bk_c, bv_c, glu_c = _premix(xc, mod, b, params, l, None)

        oa = _mla(qa, [(ka, va), (ka_c, va_c)])
        ob = _swa(bq, bk, bv, bk_c, bv_c, params, l)
        x1, h2 = _merge(oa, ob, glu, x, mod, None, params, l)
        x = _mlp(h2, x1, mod, None, params, l, None if update_ctx else final_norm.reshape(1, d))

        if update_ctx:
            oa_c = _mla(qa_c, [(ka_c, va_c)])
            ob_c = _swa(bq_c, None, None, bk_c, bv_c, params, l)
            xc1, h2c = _merge(oa_c, ob_c, glu_c, xc, mod, b, params, l)
            flat = lambda a: a.reshape(1, b * n_ctx, d)
            xc = _mlp(flat(h2c), flat(xc1), mod, b, params, l).reshape(b, n_ctx, d)
    return x
```

```python
import functools

import jax
import jax.numpy as jnp
from jax import lax
from jax.experimental import pallas as pl
from jax.experimental.pallas import tpu as pltpu

F32 = jnp.float32
BF16 = jnp.bfloat16

EPS = 1e-6
NEG_INF = -1e30
GRID_W = 64
ROPE_THETA = 10000.0

MLA_HEADS = 8
MLA_RANK = 512
MLA_NOPE = 128
MLA_ROPE = 64
MLA_V = 128
MLA_QK_PAD = 256
LOG2E = 1.4426950408889634
MLA_Q_SCALE = (MLA_NOPE + MLA_ROPE) ** -0.5 * LOG2E
SWA_HEADS = 8
SWA_KV_HEADS = 2
SWA_GROUP = SWA_HEADS // SWA_KV_HEADS
SWA_DH = 64
SWA_WINDOW = 128
SWA_BLOCK = 128
SWA_Q_SCALE = SWA_DH ** -0.5 * LOG2E
CONV_CH = 512
CONV_K = 31
CONV_HALO = 16

OFF_AQ = 0
OFF_AKV = 512
OFF_KR = 1024
IN_PAD_WIDTH = 2944
TAIL_BQ = 0
TAIL_BK = 512
TAIL_BV = 640
TAIL_C = 768

ROW_TILE = 256
PREMIX_ROWS = 512
MLP_ROWS = 512
MLP_FF_TILE = 1024
MLA_Q_TILE = 256
MLA_GROUP = 4
MLA_STEP_ROWS = 2048
MLA_SM_ROWS = 16
SWA_STEP_BLOCKS = 8
ADA_N_TILE = 1024
VMEM_LIMIT = 56 * 1024 * 1024


def _cparams(sem):
    return pltpu.CompilerParams(dimension_semantics=sem, vmem_limit_bytes=VMEM_LIMIT)


def _layer_spec(stack, l):
    _, a, b = stack.shape
    return pl.BlockSpec((None, a, b), lambda *_: (l, 0, 0), pipeline_mode=pl.Buffered(1))


def _mod_spec(mod, l, row):
    d = mod.shape[-1]
    return pl.BlockSpec((None, None, 6, d), lambda i, *_: (l, i if row is None else row, 0, 0))


def _rms(x, g):
    return x * lax.rsqrt(jnp.mean(x * x, axis=-1, keepdims=True) + EPS) * g


def _dot(a, b):
    return jnp.dot(a, b, preferred_element_type=F32)


def _dot_nt(a, b):
    return lax.dot_general(a, b, (((1,), (1,)), ((), ())), preferred_element_type=F32)


def _ada_kernel(c_ref, w_ref, b_ref, o_ref):
    c = c_ref[...]
    s = c * jax.nn.sigmoid(c)
    o_ref[0] = _dot(s.astype(BF16), w_ref[0].astype(BF16)) + b_ref[0]


def _ada(cvec, ada_w, ada_b):
    n_layers, d, n = ada_w.shape
    rows = cvec.shape[0]
    tn = ADA_N_TILE
    return pl.pallas_call(
        _ada_kernel,
        grid=(n_layers, n // tn),
        in_specs=[
            pl.BlockSpec((rows, d), lambda l, j: (0, 0)),
            pl.BlockSpec((1, d, tn), lambda l, j: (l, 0, j)),
            pl.BlockSpec((1, 1, tn), lambda l, j: (l, 0, j)),
        ],
        out_specs=pl.BlockSpec((1, rows, tn), lambda l, j: (l, 0, j)),
        out_shape=jax.ShapeDtypeStruct((n_layers, rows, n), F32),
        compiler_params=_cparams(("arbitrary", "arbitrary")),
        name="ada",
    )(cvec, ada_w, ada_b.reshape(n_layers, 1, n))


def _rope128(t, cos, sin):
    lane = lax.broadcasted_iota(jnp.int32, t.shape, 1)
    up = pltpu.roll(t, 128 - 16, 1)
    dn = pltpu.roll(t, 16, 1)
    sw = jnp.where((lane & 16) == 0, up, dn)
    return t * cos + sw * sin


def _premix_kernel(*refs, use_rope):
    if use_rope:
        (x_ref, mod_ref, g_ref, win_ref, qn_ref, wuq_ref, kvn_ref, wukv_ref, cos_ref, sin_ref,
         qa_ref, ka_ref, va_ref, bq_ref, bk_ref, bv_ref, glu_ref) = refs
        cos = cos_ref[...]
        sin = sin_ref[...]
        rope = lambda t: _rope128(t, cos, sin)
    else:
        (x_ref, mod_ref, g_ref, win_ref, qn_ref, wuq_ref, kvn_ref, wukv_ref,
         qa_ref, ka_ref, va_ref, bq_ref, bk_ref, bv_ref, glu_ref) = refs
        rope = lambda t: t

    m = mod_ref[...]
    h = _rms(x_ref[0], g_ref[...]) * (1.0 + m[1:2]) + m[0:1]
    p = _dot(h.astype(BF16), win_ref[...])
    tail = pltpu.roll(p[:, OFF_KR:], IN_PAD_WIDTH - OFF_KR - MLA_ROPE, 1)
    lane = lax.broadcasted_iota(jnp.int32, (p.shape[0], 128), 1)
    kr = jnp.where(lane < MLA_ROPE, p[:, OFF_KR:OFF_KR + 128], 0.0)

    qn = _rms(p[:, OFF_AQ:OFF_AQ + MLA_RANK], qn_ref[...])
    q = _dot(qn.astype(BF16), wuq_ref[...])
    for hd in range(MLA_HEADS):
        c0 = hd * MLA_QK_PAD
        qa_ref[0, hd, :, 0:128] = (q[:, c0:c0 + 128] * MLA_Q_SCALE).astype(BF16)
        qa_ref[0, hd, :, 128:256] = (rope(q[:, c0 + 128:c0 + 256]) * MLA_Q_SCALE).astype(BF16)

    kvn = _rms(p[:, OFF_AKV:OFF_AKV + MLA_RANK], kvn_ref[...])
    kv = _dot(kvn.astype(BF16), wukv_ref[...])
    kpe = rope(kr).astype(BF16)
    for hd in range(MLA_HEADS):
        c0 = hd * (MLA_NOPE + MLA_V)
        ka_ref[0, hd, :, 0:128] = kv[:, c0:c0 + 128].astype(BF16)
        ka_ref[0, hd, :, 128:256] = kpe
        va_ref[0, hd] = kv[:, c0 + 128:c0 + 256].astype(BF16)

    for t in range(SWA_HEADS * SWA_DH // 128):
        c0 = TAIL_BQ + t * 128
        bq_ref[0, :, t * 128:(t + 1) * 128] = (rope(tail[:, c0:c0 + 128]) * SWA_Q_SCALE).astype(BF16)
    bk_ref[0] = rope(tail[:, TAIL_BK:TAIL_BK + 128]).astype(BF16)
    bv_ref[0] = tail[:, TAIL_BV:TAIL_BV + 128].astype(BF16)

    glu_ref[0] = (tail[:, TAIL_C:TAIL_C + CONV_CH]
                  * jax.nn.sigmoid(tail[:, TAIL_C + CONV_CH:TAIL_C + 2 * CONV_CH]))


def _premix(x, mod, mod_row, params, l, rope_tabs):
    b, t, d = x.shape
    tm = min(PREMIX_ROWS, t)
    use_rope = rope_tabs is not None
    stacks = [params[k] for k in ("norm_mix", "w_in", "q_norm", "w_uq", "kv_norm", "w_ukv")]
    in_specs = [pl.BlockSpec((1, tm, d), lambda i, j: (i, j, 0)), _mod_spec(mod, l, mod_row)]
    in_specs += [_layer_spec(a, l) for a in stacks]
    args = [x, mod] + stacks
    if use_rope:
        in_specs += [pl.BlockSpec((tm, 128), lambda i, j: (j, 0))] * 2
        args += list(rope_tabs)
    hq = MLA_HEADS
    out_shape = (
        jax.ShapeDtypeStruct((b, hq, t, MLA_QK_PAD), BF16),
        jax.ShapeDtypeStruct((b, hq, t, MLA_QK_PAD), BF16),
        jax.ShapeDtypeStruct((b, hq, t, MLA_V), BF16),
        jax.ShapeDtypeStruct((b, t, SWA_HEADS * SWA_DH), BF16),
        jax.ShapeDtypeStruct((b, t, SWA_KV_HEADS * SWA_DH), BF16),
        jax.ShapeDtypeStruct((b, t, SWA_KV_HEADS * SWA_DH), BF16),
        jax.ShapeDtypeStruct((b, t, CONV_CH), F32),
    )
    out_specs = (
        pl.BlockSpec((1, hq, tm, MLA_QK_PAD), lambda i, j: (i, 0, j, 0)),
        pl.BlockSpec((1, hq, tm, MLA_QK_PAD), lambda i, j: (i, 0, j, 0)),
        pl.BlockSpec((1, hq, tm, MLA_V), lambda i, j: (i, 0, j, 0)),
        pl.BlockSpec((1, tm, SWA_HEADS * SWA_DH), lambda i, j: (i, j, 0)),
        pl.BlockSpec((1, tm, SWA_KV_HEADS * SWA_DH), lambda i, j: (i, j, 0)),
        pl.BlockSpec((1, tm, SWA_KV_HEADS * SWA_DH), lambda i, j: (i, j, 0)),
        pl.BlockSpec((1, tm, CONV_CH), lambda i, j: (i, j, 0)),
    )
    return pl.pallas_call(
        functools.partial(_premix_kernel, use_rope=use_rope),
        grid=(b, t // tm),
        in_specs=in_specs,
        out_specs=out_specs,
        out_shape=out_shape,
        compiler_params=_cparams(("arbitrary", "arbitrary")),
        name="premix_rope" if use_rope else "premix",
    )(*args)


def _mla_kernel(*refs, src_lens, tile):
    n_src = len(src_lens)
    q_ref = refs[0]
    kv_refs = refs[1:1 + 2 * n_src]
    o_ref = refs[1 + 2 * n_src]
    scratch = refs[2 + 2 * n_src:]
    group = len(scratch) // 3
    s_refs, p_refs, l_refs = scratch[:group], scratch[group:2 * group], scratch[2 * group:]
    offs = [sum(src_lens[:i]) for i in range(n_src)]
    n_groups = q_ref.shape[2] // (group * tile)

    def one_group(gi):
        def rows(t):
            start = (group * gi + t) * tile
            return pl.ds(start if isinstance(gi, int) else pl.multiple_of(start, tile), tile)

        _mla_group(q_ref, kv_refs, o_ref, s_refs, p_refs, l_refs, rows, src_lens, offs, tile)

    if n_groups == 1:
        one_group(0)
    else:
        pl.loop(0, n_groups)(one_group)


def _mla_group(q_ref, kv_refs, o_ref, s_refs, p_refs, l_refs, rows, src_lens, offs, tile):
    n_src = len(src_lens)
    group = len(s_refs)

    def scores(t):
        qt = q_ref[0, 0, rows(t), :]
        for si in range(n_src):
            s_refs[t][:, offs[si]:offs[si] + src_lens[si]] = _dot_nt(qt, kv_refs[2 * si][0, 0])

    def softmax(t):
        for r in range(0, tile, MLA_SM_ROWS):
            rs = slice(r, r + MLA_SM_ROWS)
            sb = s_refs[t][rs, :]
            p = jnp.exp2(sb - jnp.max(sb, axis=1, keepdims=True))
            l_refs[t][rs, :] = jnp.broadcast_to(jnp.sum(p, axis=1, keepdims=True), (MLA_SM_ROWS, 128))
            p_refs[t][rs, :] = p.astype(BF16)

    def values(t):
        acc = None
        for si in range(n_src):
            part = _dot(p_refs[t][:, offs[si]:offs[si] + src_lens[si]], kv_refs[2 * si + 1][0, 0])
            acc = part if acc is None else acc + part
        o_ref[0, rows(t), :] = acc / l_refs[t][...]

    scores(0)
    for t in range(group):
        if t + 1 < group:
            scores(t + 1)
        softmax(t)
        values(t)


def _mla(q, kv_sources):
    b, hq, tq_all, dq = q.shape
    tile = min(MLA_Q_TILE, tq_all // 2)
    tq = min(MLA_STEP_ROWS, tq_all)
    group = min(MLA_GROUP, tq // tile)
    assert tq_all % tq == 0 and tq % (group * tile) == 0
    in_specs = [pl.BlockSpec((1, 1, tq, dq), lambda i, h, j: (i, h, j, 0))]
    args = [q]
    src_lens = []
    for k, v in kv_sources:
        n = k.shape[2]
        src_lens.append(n)
        in_specs.append(pl.BlockSpec((1, 1, n, dq), lambda i, h, j: (i, h, 0, 0)))
        in_specs.append(pl.BlockSpec((1, 1, n, MLA_V), lambda i, h, j: (i, h, 0, 0)))
        args += [k, v]
    n_keys = sum(src_lens)
    return pl.pallas_call(
        functools.partial(_mla_kernel, src_lens=tuple(src_lens), tile=tile),
        grid=(b, hq, tq_all // tq),
        in_specs=in_specs,
        out_specs=pl.BlockSpec((1, tq, MLA_V), lambda i, h, j: (i, j, h)),
        out_shape=jax.ShapeDtypeStruct((b, tq_all, hq * MLA_V), F32),
        scratch_shapes=(
            [pltpu.VMEM((tile, n_keys), F32)] * group
            + [pltpu.VMEM((tile, n_keys), BF16)] * group
            + [pltpu.VMEM((tile, 128), F32)] * group),
        compiler_params=_cparams(("arbitrary", "arbitrary", "arbitrary")),
        name="mla_%d" % len(kv_sources),
    )(*args)


def _swa_kernel(*refs, s_len, nblk):
    latent = s_len > 0
    if latent:
        q_ref, kl_ref, vl_ref, kc_ref, vc_ref, sink_ref, o_ref = refs
    else:
        q_ref, kc_ref, vc_ref, sink_ref, o_ref = refs
    kc = kc_ref[0]
    vc = vc_ref[0]
    rows = SWA_GROUP * SWA_BLOCK
    win = 3 * SWA_BLOCK
    chains = [(blk, kh) for blk in range(nblk) for kh in range(SWA_KV_HEADS)]
    kcat, vcat, valid = {}, {}, {}
    n_win_tiles = win // 128 if latent else 0
    if latent:
        d = (lax.broadcasted_iota(jnp.int32, (rows, win), 1)
             - (lax.broadcasted_iota(jnp.int32, (rows, win), 0) & (SWA_BLOCK - 1)))
        for blk in range(nblk):
            n = pl.program_id(1) * nblk + blk
            start = jnp.clip((n - 1) * SWA_BLOCK, 0, s_len - win)
            start = pl.multiple_of(start, SWA_BLOCK)
            kcat[blk] = jnp.concatenate([kl_ref[0, pl.ds(start, win), :], kc], axis=0)
            vcat[blk] = jnp.concatenate([vl_ref[0, pl.ds(start, win), :], vc], axis=0)
            valid[blk] = jnp.abs(d + (start - n * SWA_BLOCK)) <= SWA_WINDOW
    else:
        for blk in range(nblk):
            kcat[blk], vcat[blk] = kc, vc

    s, snk, mx = {}, {}, {}
    for c in chains:
        blk, kh = c
        heads = range(kh * SWA_GROUP, (kh + 1) * SWA_GROUP)
        lo, hi = kh * SWA_DH, (kh + 1) * SWA_DH
        q = q_ref[0, blk * SWA_BLOCK:(blk + 1) * SWA_BLOCK, :]
        qs = jnp.concatenate([q[:, h * SWA_DH:(h + 1) * SWA_DH] for h in heads], axis=0)
        snk[c] = jnp.concatenate(
            [jnp.broadcast_to(sink_ref[h:h + 1, 0:1] * LOG2E, (SWA_BLOCK, 1)) for h in heads], axis=0)
        sc = _dot_nt(qs, kcat[blk][:, lo:hi])
        tiles = [sc[:, j * 128:(j + 1) * 128] for j in range(sc.shape[1] // 128)]
        for j in range(n_win_tiles):
            tiles[j] = jnp.where(valid[blk][:, j * 128:(j + 1) * 128], tiles[j], NEG_INF)
        s[c] = tiles
    for c in chains:
        mm = functools.reduce(jnp.maximum, s[c])
        mx[c] = jnp.maximum(jnp.max(mm, axis=1, keepdims=True), snk[c])
    prob, den = {}, {}
    for c in chains:
        ps = [jnp.exp2(t - mx[c]) for t in s[c]]
        den[c] = jnp.sum(functools.reduce(jnp.add, ps), axis=1, keepdims=True) + jnp.exp2(snk[c] - mx[c])
        prob[c] = jnp.concatenate([p.astype(BF16) for p in ps], axis=1)
    for c in chains:
        blk, kh = c
        lo, hi = kh * SWA_DH, (kh + 1) * SWA_DH
        o = _dot(prob[c], vcat[blk][:, lo:hi]) / den[c]
        for g in range(SWA_GROUP):
            h = kh * SWA_GROUP + g
            o_ref[0, blk * SWA_BLOCK:(blk + 1) * SWA_BLOCK, h * SWA_DH:(h + 1) * SWA_DH] = (
                o[g * SWA_BLOCK:(g + 1) * SWA_BLOCK])


def _swa(q, k_lat, v_lat, k_ctx, v_ctx, params, l):
    sink_b = params["sink"]
    b, tq_all, dq = q.shape
    dkv = k_ctx.shape[2]
    n_ctx = k_ctx.shape[1]
    latent = k_lat is not None
    nblk = min(SWA_STEP_BLOCKS, tq_all // SWA_BLOCK)
    tq = nblk * SWA_BLOCK
    in_specs = [pl.BlockSpec((1, tq, dq), lambda i, j: (i, j, 0))]
    args = [q]
    if latent:
        s_len = k_lat.shape[1]
        in_specs += [pl.BlockSpec((1, s_len, dkv), lambda i, j: (i, 0, 0))] * 2
        args += [k_lat, v_lat]
    else:
        s_len = 0
    in_specs += [pl.BlockSpec((1, n_ctx, dkv), lambda i, j: (i, 0, 0))] * 2
    in_specs += [_layer_spec(sink_b, l)]
    args += [k_ctx, v_ctx, sink_b]
    return pl.pallas_call(
        functools.partial(_swa_kernel, s_len=s_len, nblk=nblk),
        grid=(b, tq_all // tq),
        in_specs=in_specs,
        out_specs=pl.BlockSpec((1, tq, dq), lambda i, j: (i, j, 0)),
        out_shape=jax.ShapeDtypeStruct((b, tq_all, dq), F32),
        compiler_params=_cparams(("arbitrary", "arbitrary")),
        name="swa_lat" if latent else "swa_ctx",
    )(*args)


CONV_ROWS = 64


def _conv_rows(prev_ref, cur_ref, next_ref, w_ref, b_ref, lg_ref, lb_ref, ext_ref, sh_ref, nt):
    j = pl.program_id(1)
    tm = cur_ref.shape[1]
    hl = CONV_HALO
    ext_ref[0:hl] = jnp.where(j > 0, prev_ref[0, tm - hl:tm, :], 0.0)
    ext_ref[hl:hl + tm] = cur_ref[0]
    ext_ref[hl + tm:2 * hl + tm] = jnp.where(j < nt - 1, next_ref[0, 0:hl, :], 0.0)
    n_sh = sh_ref.shape[1]
    for sb in range(8):
        sh_ref[sb] = ext_ref[sb:sb + n_sh, :]
    off = hl - CONV_K // 2
    for r0 in range(0, tm, CONV_ROWS):
        acc = None
        for k in range(CONV_K):
            sb, a8 = (off + k) % 8, (off + k) // 8 * 8
            tap = sh_ref[sb, r0 + a8:r0 + a8 + CONV_ROWS, :].reshape(CONV_ROWS // 8, 8, -1)
            term = (tap * w_ref[k * 8:(k + 1) * 8, :][None]).reshape(CONV_ROWS, -1)
            acc = term if acc is None else acc + term
        hcv = acc + b_ref[...]
        mu = jnp.mean(hcv, axis=-1, keepdims=True)
        xc = hcv - mu
        y = xc * lax.rsqrt(jnp.mean(xc * xc, axis=-1, keepdims=True) + EPS) * lg_ref[...] + lb_ref[...]
        yield r0, y * jax.nn.sigmoid(y)


def _merge_kernel(oa_ref, ob_ref, gp_ref, gc_ref, gn_ref, cw_ref, cb_ref, lg_ref, lb_ref,
                  on_ref, wout_ref, x_ref, mod_ref, gm_ref, x1_ref, h2_ref, ext_ref, sh_ref, yc_ref, *, nt):
    on = on_ref[...]
    na = oa_ref.shape[2]
    nb = ob_ref.shape[2]
    yab = jnp.concatenate([_rms(oa_ref[0], on[:, 0:na]), _rms(ob_ref[0], on[:, na:na + nb])],
                          axis=1).astype(BF16)
    proj = _dot(yab, wout_ref[0:na + nb, :])
    for r0, oc in _conv_rows(gp_ref, gc_ref, gn_ref, cw_ref, cb_ref, lg_ref, lb_ref, ext_ref, sh_ref, nt):
        yc_ref[r0:r0 + CONV_ROWS, :] = _rms(oc, on[:, na + nb:]).astype(BF16)
    proj = proj + _dot(yc_ref[...], wout_ref[na + nb:, :])
    m = mod_ref[...]
    x1 = x_ref[0] + m[2:3] * proj
    x1_ref[0] = x1
    h2_ref[0] = (_rms(x1, gm_ref[...]) * (1.0 + m[4:5]) + m[3:4]).astype(BF16)


def _merge(oa, ob, glu, x, mod, mod_row, params, l):
    b, t, d = x.shape
    ch = glu.shape[2]
    tm = ROW_TILE
    nt = t // tm
    row = lambda i, j: (i, j, 0)
    stacks = [params[k] for k in ("conv_w", "conv_b", "conv_ln_g", "conv_ln_b", "out_norm", "w_out")]
    g_mlp = params["norm_mlp"]
    return pl.pallas_call(
        functools.partial(_merge_kernel, nt=nt),
        grid=(b, nt),
        in_specs=[
            pl.BlockSpec((1, tm, oa.shape[2]), row),
            pl.BlockSpec((1, tm, ob.shape[2]), row),
            pl.BlockSpec((1, tm, ch), lambda i, j: (i, jnp.maximum(j - 1, 0), 0)),
            pl.BlockSpec((1, tm, ch), row),
            pl.BlockSpec((1, tm, ch), lambda i, j: (i, jnp.minimum(j + 1, nt - 1), 0)),
        ] + [_layer_spec(a, l) for a in stacks] + [
            pl.BlockSpec((1, tm, d), row),
            _mod_spec(mod, l, mod_row),
            _layer_spec(g_mlp, l),
        ],
        out_specs=(pl.BlockSpec((1, tm, d), row), pl.BlockSpec((1, tm, d), row)),
        out_shape=(jax.ShapeDtypeStruct((b, t, d), F32), jax.ShapeDtypeStruct((b, t, d), BF16)),
        scratch_shapes=[pltpu.VMEM((tm + 2 * CONV_HALO, ch), F32),
                        pltpu.VMEM((8, tm + 2 * CONV_HALO - 8, ch), F32),
                        pltpu.VMEM((tm, ch), BF16)],
        compiler_params=_cparams(("arbitrary", "arbitrary")),
        name="merge",
    )(oa, ob, glu, glu, glu, *stacks, x, mod, g_mlp)


def _mlp_kernel(*refs, nf, final):
    if final:
        h_ref, w1_ref, w2_ref, x_ref, mod_ref, fn_ref, o_ref, acc_ref = refs
    else:
        h_ref, w1_ref, w2_ref, x_ref, mod_ref, o_ref, acc_ref = refs
    j = pl.program_id(2)

    @pl.when(j == 0)
    def _():
        acc_ref[...] = jnp.zeros_like(acc_ref)

    a = jnp.square(jnp.maximum(_dot(h_ref[0], w1_ref[...]), 0.0))
    acc_ref[...] += _dot(a.astype(BF16), w2_ref[...])

    @pl.when(j == nf - 1)
    def _():
        out = x_ref[0] + mod_ref[5:6, :] * acc_ref[...]
        if final:
            out = _rms(out, fn_ref[...])
        o_ref[0] = out


def _mlp(h2, x1, mod, mod_row, params, l, final_norm=None):
    b, t, d = x1.shape
    w1, w2 = params["w1"], params["w2"]
    dff = w1.shape[2]
    tr = min(MLP_ROWS, t)
    tf = MLP_FF_TILE
    nf = dff // tf
    final = final_norm is not None
    row = lambda i, r, j: (i, r, 0)
    in_specs = [
        pl.BlockSpec((1, tr, d), row),
        pl.BlockSpec((None, d, tf), lambda i, r, j: (l, 0, j)),
        pl.BlockSpec((None, tf, d), lambda i, r, j: (l, j, 0)),
        pl.BlockSpec((1, tr, d), row),
        _mod_spec(mod, l, mod_row),
    ]
    args = [h2, w1, w2, x1, mod]
    if final:
        in_specs.append(pl.BlockSpec(final_norm.shape, lambda i, r, j: (0, 0)))
        args.append(final_norm)
    return pl.pallas_call(
        functools.partial(_mlp_kernel, nf=nf, final=final),
        grid=(b, t // tr, nf),
        in_specs=in_specs,
        out_specs=pl.BlockSpec((1, tr, d), row),
        out_shape=jax.ShapeDtypeStruct((b, t, d), F32),
        scratch_shapes=[pltpu.VMEM((tr, d), F32)],
        compiler_params=_cparams(("arbitrary", "arbitrary", "arbitrary")),
        name="mlp_final" if final else "mlp",
    )(*args)


def _rope_tables(n_tok):
    rows = n_tok // GRID_W
    row = jnp.repeat(jnp.arange(rows, dtype=F32), GRID_W)
    col = jnp.tile(jnp.arange(GRID_W, dtype=F32), rows)
    n_freq = MLA_ROPE // 4
    inv_freq = ROPE_THETA ** (-jnp.arange(n_freq, dtype=F32) / n_freq)
    ar = row[:, None] * inv_freq
    ac = col[:, None] * inv_freq
    cos = jnp.concatenate([jnp.cos(ar), jnp.cos(ar), jnp.cos(ac), jnp.cos(ac)], axis=1)
    sin = jnp.concatenate([-jnp.sin(ar), jnp.sin(ar), -jnp.sin(ac), jnp.sin(ac)], axis=1)
    return jnp.tile(cos, (1, 2)), jnp.tile(sin, (1, 2))


def kernel(x, c, ctx, c_ctx, ada_w, ada_b, norm_mix, norm_mlp, w_in, mla_q_norm, mla_w_uq, mla_kv_norm, mla_w_ukv, swa_sink, conv_w, conv_b, conv_ln_g, conv_ln_b, out_norm, w_out, mlp_w1, mlp_w2, final_norm):
    b, s, d = x.shape
    n_ctx = ctx.shape[1]
    depth = ada_w.shape[0]
    assert s % (2 * MLA_Q_TILE) == 0 and s % MLP_ROWS == 0 and n_ctx % ROW_TILE == 0
    assert s % (SWA_STEP_BLOCKS * SWA_BLOCK) == 0
    assert s >= 3 * SWA_BLOCK and b + 1 <= 8

    cvec = jnp.concatenate([c, c_ctx[None, :], jnp.zeros((8 - b - 1, d), F32)], axis=0)
    mod = _ada(cvec, ada_w, ada_b).reshape(depth, 8, 6, d)
    rope_tabs = _rope_tables(s)

    vec = lambda v: v.reshape(depth, 1, -1)
    params = {
        "norm_mix": vec(norm_mix),
        "w_in": jnp.pad(w_in.astype(BF16), ((0, 0), (0, 0), (0, IN_PAD_WIDTH - w_in.shape[2]))),
        "q_norm": vec(mla_q_norm),
        "w_uq": jnp.pad(mla_w_uq.astype(BF16), ((0, 0), (0, 0), (0, 0), (0, MLA_QK_PAD - MLA_NOPE - MLA_ROPE))
                        ).reshape(depth, MLA_RANK, MLA_HEADS * MLA_QK_PAD),
        "kv_norm": vec(mla_kv_norm),
        "w_ukv": mla_w_ukv.astype(BF16).reshape(depth, MLA_RANK, MLA_HEADS * (MLA_NOPE + MLA_V)),
        "sink": jnp.broadcast_to(swa_sink[:, :, None], (depth, SWA_HEADS, 128)),
        "conv_w": jnp.broadcast_to(conv_w.reshape(depth, CONV_K, 1, CONV_CH), (depth, CONV_K, 8, CONV_CH)
                                   ).reshape(depth, CONV_K * 8, CONV_CH),
        "conv_b": vec(conv_b),
        "conv_ln_g": vec(conv_ln_g),
        "conv_ln_b": vec(conv_ln_b),
        "out_norm": vec(out_norm),
        "w_out": w_out.astype(BF16),
        "norm_mlp": vec(norm_mlp),
        "w1": mlp_w1.astype(BF16),
        "w2": mlp_w2.astype(BF16),
    }

    xc = ctx
    for l in range(depth):
        update_ctx = l < depth - 1
        qa, ka, va, bq, bk, bv, glu = _premix(x, mod, None, params, l, rope_tabs)
        qa_c, ka_c, va_c, bq_c, bk_c, bv_c, glu_c = _premix(xc, mod, b, params, l, None)

        oa = _mla(qa, [(ka, va), (ka_c, va_c)])
        ob = _swa(bq, bk, bv, bk_c, bv_c, params, l)
        x1, h2 = _merge(oa, ob, glu, x, mod, None, params, l)
        x = _mlp(h2, x1, mod, None, params, l, None if update_ctx else final_norm.reshape(1, d))

        if update_ctx:
            oa_c = _mla(qa_c, [(ka_c, va_c)])
            ob_c = _swa(bq_c, None, None, bk_c, bv_c, params, l)
            xc1, h2c = _merge(oa_c, ob_c, glu_c, xc, mod, b, params, l)
            flat = lambda a: a.reshape(1, b * n_ctx, d)
            xc = _mlp(flat(h2c), flat(xc1), mod, b, params, l).reshape(b, n_ctx, d)
    return x
```

```python
import functools

import jax
import jax.numpy as jnp
from jax import lax
from jax.experimental import pallas as pl
from jax.experimental.pallas import tpu as pltpu

F32 = jnp.float32
BF16 = jnp.bfloat16

EPS = 1e-6
NEG_INF = -1e30
GRID_W = 64
ROPE_THETA = 10000.0

MLA_HEADS = 8
MLA_RANK = 512
MLA_NOPE = 128
MLA_ROPE = 64
MLA_V = 128
MLA_QK_PAD = 256
LOG2E = 1.4426950408889634
MLA_Q_SCALE = (MLA_NOPE + MLA_ROPE) ** -0.5 * LOG2E
SWA_HEADS = 8
SWA_KV_HEADS = 2
SWA_GROUP = SWA_HEADS // SWA_KV_HEADS
SWA_DH = 64
SWA_WINDOW = 128
SWA_BLOCK = 128
SWA_Q_SCALE = SWA_DH ** -0.5 * LOG2E
CONV_CH = 512
CONV_K = 31
CONV_HALO = 16

OFF_AQ = 0
OFF_AKV = 512
OFF_KR = 1024
IN_PAD_WIDTH = 2944
TAIL_BQ = 0
TAIL_BK = 512
TAIL_BV = 640
TAIL_C = 768

ROW_TILE = 256
MERGE_ROWS = 512
MERGE_HALF = 256
PREMIX_ROWS = 512
MLP_ROWS = 512
MLP_FF_TILE = 1024
MLA_Q_TILE = 256
MLA_GROUP = 4
MLA_STEP_ROWS = 2048
MLA_SM_ROWS = 16
SWA_STEP_BLOCKS = 8
ADA_N_TILE = 1024
VMEM_LIMIT = 56 * 1024 * 1024


def _cparams(sem):
    return pltpu.CompilerParams(dimension_semantics=sem, vmem_limit_bytes=VMEM_LIMIT)


def _layer_spec(stack, l):
    _, a, b = stack.shape
    return pl.BlockSpec((None, a, b), lambda *_: (l, 0, 0), pipeline_mode=pl.Buffered(1))


def _mod_spec(mod, l, row):
    d = mod.shape[-1]
    return pl.BlockSpec((None, None, 6, d), lambda i, *_: (l, i if row is None else row, 0, 0))


def _rms(x, g):
    return x * lax.rsqrt(jnp.mean(x * x, axis=-1, keepdims=True) + EPS) * g


def _dot(a, b):
    return jnp.dot(a, b, preferred_element_type=F32)


def _dot_nt(a, b):
    return lax.dot_general(a, b, (((1,), (1,)), ((), ())), preferred_element_type=F32)


def _ada_kernel(c_ref, w_ref, b_ref, o_ref):
    c = c_ref[...]
    s = c * jax.nn.sigmoid(c)
    o_ref[0] = _dot(s.astype(BF16), w_ref[0].astype(BF16)) + b_ref[0]


def _ada(cvec, ada_w, ada_b):
    n_layers, d, n = ada_w.shape
    rows = cvec.shape[0]
    tn = ADA_N_TILE
    return pl.pallas_call(
        _ada_kernel,
        grid=(n_layers, n // tn),
        in_specs=[
            pl.BlockSpec((rows, d), lambda l, j: (0, 0)),
            pl.BlockSpec((1, d, tn), lambda l, j: (l, 0, j)),
            pl.BlockSpec((1, 1, tn), lambda l, j: (l, 0, j)),
        ],
        out_specs=pl.BlockSpec((1, rows, tn), lambda l, j: (l, 0, j)),
        out_shape=jax.ShapeDtypeStruct((n_layers, rows, n), F32),
        compiler_params=_cparams(("arbitrary", "arbitrary")),
        name="ada",
    )(cvec, ada_w, ada_b.reshape(n_layers, 1, n))


def _rope128(t, cos, sin):
    lane = lax.broadcasted_iota(jnp.int32, t.shape, 1)
    up = pltpu.roll(t, 128 - 16, 1)
    dn = pltpu.roll(t, 16, 1)
    sw = jnp.where((lane & 16) == 0, up, dn)
    return t * cos + sw * sin


def _premix_kernel(*refs, use_rope):
    if use_rope:
        (x_ref, mod_ref, g_ref, win_ref, qn_ref, wuq_ref, kvn_ref, wukv_ref, cos_ref, sin_ref,
         qa_ref, ka_ref, va_ref, bq_ref, bk_ref, bv_ref, glu_ref) = refs
        cos = cos_ref[...]
        sin = sin_ref[...]
        rope = lambda t: _rope128(t, cos, sin)
    else:
        (x_ref, mod_ref, g_ref, win_ref, qn_ref, wuq_ref, kvn_ref, wukv_ref,
         qa_ref, ka_ref, va_ref, bq_ref, bk_ref, bv_ref, glu_ref) = refs
        rope = lambda t: t

    m = mod_ref[...]
    h = _rms(x_ref[0], g_ref[...]) * (1.0 + m[1:2]) + m[0:1]
    p = _dot(h.astype(BF16), win_ref[...])
    tail = pltpu.roll(p[:, OFF_KR:], IN_PAD_WIDTH - OFF_KR - MLA_ROPE, 1)
    lane = lax.broadcasted_iota(jnp.int32, (p.shape[0], 128), 1)
    kr = jnp.where(lane < MLA_ROPE, p[:, OFF_KR:OFF_KR + 128], 0.0)

    qn = _rms(p[:, OFF_AQ:OFF_AQ + MLA_RANK], qn_ref[...])
    q = _dot(qn.astype(BF16), wuq_ref[...])
    for hd in range(MLA_HEADS):
        c0 = hd * MLA_QK_PAD
        qa_ref[0, hd, :, 0:128] = (q[:, c0:c0 + 128] * MLA_Q_SCALE).astype(BF16)
        qa_ref[0, hd, :, 128:256] = (rope(q[:, c0 + 128:c0 + 256]) * MLA_Q_SCALE).astype(BF16)

    kvn = _rms(p[:, OFF_AKV:OFF_AKV + MLA_RANK], kvn_ref[...])
    kv = _dot(kvn.astype(BF16), wukv_ref[...])
    kpe = rope(kr).astype(BF16)
    for hd in range(MLA_HEADS):
        c0 = hd * (MLA_NOPE + MLA_V)
        ka_ref[0, hd, :, 0:128] = kv[:, c0:c0 + 128].astype(BF16)
        ka_ref[0, hd, :, 128:256] = kpe
        va_ref[0, hd] = kv[:, c0 + 128:c0 + 256].astype(BF16)

    for t in range(SWA_HEADS * SWA_DH // 128):
        c0 = TAIL_BQ + t * 128
        bq_ref[0, :, t * 128:(t + 1) * 128] = (rope(tail[:, c0:c0 + 128]) * SWA_Q_SCALE).astype(BF16)
    bk_ref[0] = rope(tail[:, TAIL_BK:TAIL_BK + 128]).astype(BF16)
    bv_ref[0] = tail[:, TAIL_BV:TAIL_BV + 128].astype(BF16)

    glu_ref[0] = (tail[:, TAIL_C:TAIL_C + CONV_CH]
                  * jax.nn.sigmoid(tail[:, TAIL_C + CONV_CH:TAIL_C + 2 * CONV_CH]))


def _premix(x, mod, mod_row, params, l, rope_tabs):
    b, t, d = x.shape
    tm = min(PREMIX_ROWS, t)
    use_rope = rope_tabs is not None
    stacks = [params[k] for k in ("norm_mix", "w_in", "q_norm", "w_uq", "kv_norm", "w_ukv")]
    in_specs = [pl.BlockSpec((1, tm, d), lambda i, j: (i, j, 0)), _mod_spec(mod, l, mod_row)]
    in_specs += [_layer_spec(a, l) for a in stacks]
    args = [x, mod] + stacks
    if use_rope:
        in_specs += [pl.BlockSpec((tm, 128), lambda i, j: (j, 0))] * 2
        args += list(rope_tabs)
    hq = MLA_HEADS
    out_shape = (
        jax.ShapeDtypeStruct((b, hq, t, MLA_QK_PAD), BF16),
        jax.ShapeDtypeStruct((b, hq, t, MLA_QK_PAD), BF16),
        jax.ShapeDtypeStruct((b, hq, t, MLA_V), BF16),
        jax.ShapeDtypeStruct((b, t, SWA_HEADS * SWA_DH), BF16),
        jax.ShapeDtypeStruct((b, t, SWA_KV_HEADS * SWA_DH), BF16),
        jax.ShapeDtypeStruct((b, t, SWA_KV_HEADS * SWA_DH), BF16),
        jax.ShapeDtypeStruct((b, t, CONV_CH), F32),
    )
    out_specs = (
        pl.BlockSpec((1, hq, tm, MLA_QK_PAD), lambda i, j: (i, 0, j, 0)),
        pl.BlockSpec((1, hq, tm, MLA_QK_PAD), lambda i, j: (i, 0, j, 0)),
        pl.BlockSpec((1, hq, tm, MLA_V), lambda i, j: (i, 0, j, 0)),
        pl.BlockSpec((1, tm, SWA_HEADS * SWA_DH), lambda i, j: (i, j, 0)),
        pl.BlockSpec((1, tm, SWA_KV_HEADS * SWA_DH), lambda i, j: (i, j, 0)),
        pl.BlockSpec((1, tm, SWA_KV_HEADS * SWA_DH), lambda i, j: (i, j, 0)),
        pl.BlockSpec((1, tm, CONV_CH), lambda i, j: (i, j, 0)),
    )
    return pl.pallas_call(
        functools.partial(_premix_kernel, use_rope=use_rope),
        grid=(b, t // tm),
        in_specs=in_specs,
        out_specs=out_specs,
        out_shape=out_shape,
        compiler_params=_cparams(("arbitrary", "arbitrary")),
        name="premix_rope" if use_rope else "premix",
    )(*args)


def _mla_kernel(*refs, src_lens, tile):
    n_src = len(src_lens)
    q_ref = refs[0]
    kv_refs = refs[1:1 + 2 * n_src]
    o_ref = refs[1 + 2 * n_src]
    scratch = refs[2 + 2 * n_src:]
    group = len(scratch) // 3
    s_refs, p_refs, l_refs = scratch[:group], scratch[group:2 * group], scratch[2 * group:]
    offs = [sum(src_lens[:i]) for i in range(n_src)]
    n_groups = q_ref.shape[2] // (group * tile)

    def one_group(gi):
        def rows(t):
            start = (group * gi + t) * tile
            return pl.ds(start if isinstance(gi, int) else pl.multiple_of(start, tile), tile)

        _mla_group(q_ref, kv_refs, o_ref, s_refs, p_refs, l_refs, rows, src_lens, offs, tile)

    if n_groups == 1:
        one_group(0)
    else:
        pl.loop(0, n_groups)(one_group)


def _mla_group(q_ref, kv_refs, o_ref, s_refs, p_refs, l_refs, rows, src_lens, offs, tile):
    n_src = len(src_lens)
    group = len(s_refs)

    def scores(t):
        qt = q_ref[0, 0, rows(t), :]
        for si in range(n_src):
            s_refs[t][:, offs[si]:offs[si] + src_lens[si]] = _dot_nt(qt, kv_refs[2 * si][0, 0])

    def softmax(t):
        for r in range(0, tile, MLA_SM_ROWS):
            rs = slice(r, r + MLA_SM_ROWS)
            sb = s_refs[t][rs, :]
            p = jnp.exp2(sb - jnp.max(sb, axis=1, keepdims=True))
            l_refs[t][rs, :] = jnp.broadcast_to(jnp.sum(p, axis=1, keepdims=True), (MLA_SM_ROWS, 128))
            p_refs[t][rs, :] = p.astype(BF16)

    def values(t):
        acc = None
        for si in range(n_src):
            part = _dot(p_refs[t][:, offs[si]:offs[si] + src_lens[si]], kv_refs[2 * si + 1][0, 0])
            acc = part if acc is None else acc + part
        o_ref[0, rows(t), :] = acc / l_refs[t][...]

    scores(0)
    for t in range(group):
        if t + 1 < group:
            scores(t + 1)
        softmax(t)
        values(t)


def _mla(q, kv_sources):
    b, hq, tq_all, dq = q.shape
    tile = min(MLA_Q_TILE, tq_all // 2)
    tq = min(MLA_STEP_ROWS, tq_all)
    group = min(MLA_GROUP, tq // tile)
    assert tq_all % tq == 0 and tq % (group * tile) == 0
    in_specs = [pl.BlockSpec((1, 1, tq, dq), lambda i, h, j: (i, h, j, 0))]
    args = [q]
    src_lens = []
    for k, v in kv_sources:
        n = k.shape[2]
        src_lens.append(n)
        in_specs.append(pl.BlockSpec((1, 1, n, dq), lambda i, h, j: (i, h, 0, 0)))
        in_specs.append(pl.BlockSpec((1, 1, n, MLA_V), lambda i, h, j: (i, h, 0, 0)))
        args += [k, v]
    n_keys = sum(src_lens)
    return pl.pallas_call(
        functools.partial(_mla_kernel, src_lens=tuple(src_lens), tile=tile),
        grid=(b, hq, tq_all // tq),
        in_specs=in_specs,
        out_specs=pl.BlockSpec((1, tq, MLA_V), lambda i, h, j: (i, j, h)),
        out_shape=jax.ShapeDtypeStruct((b, tq_all, hq * MLA_V), F32),
        scratch_shapes=(
            [pltpu.VMEM((tile, n_keys), F32)] * group
            + [pltpu.VMEM((tile, n_keys), BF16)] * group
            + [pltpu.VMEM((tile, 128), F32)] * group),
        compiler_params=_cparams(("arbitrary", "arbitrary", "arbitrary")),
        name="mla_%d" % len(kv_sources),
    )(*args)


def _swa_kernel(*refs, s_len, nblk):
    latent = s_len > 0
    if latent:
        q_ref, kl_ref, vl_ref, kc_ref, vc_ref, sink_ref, o_ref = refs
    else:
        q_ref, kc_ref, vc_ref, sink_ref, o_ref = refs
    kc = kc_ref[0]
    vc = vc_ref[0]
    rows = SWA_GROUP * SWA_BLOCK
    win = 3 * SWA_BLOCK
    chains = [(blk, kh) for blk in range(nblk) for kh in range(SWA_KV_HEADS)]
    kcat, vcat, valid = {}, {}, {}
    n_win_tiles = win // 128 if latent else 0
    if latent:
        d = (lax.broadcasted_iota(jnp.int32, (rows, win), 1)
             - (lax.broadcasted_iota(jnp.int32, (rows, win), 0) & (SWA_BLOCK - 1)))
        for blk in range(nblk):
            n = pl.program_id(1) * nblk + blk
            start = jnp.clip((n - 1) * SWA_BLOCK, 0, s_len - win)
            start = pl.multiple_of(start, SWA_BLOCK)
            kcat[blk] = jnp.concatenate([kl_ref[0, pl.ds(start, win), :], kc], axis=0)
            vcat[blk] = jnp.concatenate([vl_ref[0, pl.ds(start, win), :], vc], axis=0)
            valid[blk] = jnp.abs(d + (start - n * SWA_BLOCK)) <= SWA_WINDOW
    else:
        for blk in range(nblk):
            kcat[blk], vcat[blk] = kc, vc

    s, snk, mx = {}, {}, {}
    for c in chains:
        blk, kh = c
        heads = range(kh * SWA_GROUP, (kh + 1) * SWA_GROUP)
        lo, hi = kh * SWA_DH, (kh + 1) * SWA_DH
        q = q_ref[0, blk * SWA_BLOCK:(blk + 1) * SWA_BLOCK, :]
        qs = jnp.concatenate([q[:, h * SWA_DH:(h + 1) * SWA_DH] for h in heads], axis=0)
        snk[c] = jnp.concatenate(
            [jnp.broadcast_to(sink_ref[h:h + 1, 0:1] * LOG2E, (SWA_BLOCK, 1)) for h in heads], axis=0)
        sc = _dot_nt(qs, kcat[blk][:, lo:hi])
        tiles = [sc[:, j * 128:(j + 1) * 128] for j in range(sc.shape[1] // 128)]
        for j in range(n_win_tiles):
            tiles[j] = jnp.where(valid[blk][:, j * 128:(j + 1) * 128], tiles[j], NEG_INF)
        s[c] = tiles
    for c in chains:
        mm = functools.reduce(jnp.maximum, s[c])
        mx[c] = jnp.maximum(jnp.max(mm, axis=1, keepdims=True), snk[c])
    prob, den = {}, {}
    for c in chains:
        ps = [jnp.exp2(t - mx[c]) for t in s[c]]
        den[c] = jnp.sum(functools.reduce(jnp.add, ps), axis=1, keepdims=True) + jnp.exp2(snk[c] - mx[c])
        prob[c] = jnp.concatenate([p.astype(BF16) for p in ps], axis=1)
    for c in chains:
        blk, kh = c
        lo, hi = kh * SWA_DH, (kh + 1) * SWA_DH
        o = _dot(prob[c], vcat[blk][:, lo:hi]) / den[c]
        for g in range(SWA_GROUP):
            h = kh * SWA_GROUP + g
            o_ref[0, blk * SWA_BLOCK:(blk + 1) * SWA_BLOCK, h * SWA_DH:(h + 1) * SWA_DH] = (
                o[g * SWA_BLOCK:(g + 1) * SWA_BLOCK])


def _swa(q, k_lat, v_lat, k_ctx, v_ctx, params, l):
    sink_b = params["sink"]
    b, tq_all, dq = q.shape
    dkv = k_ctx.shape[2]
    n_ctx = k_ctx.shape[1]
    latent = k_lat is not None
    nblk = min(SWA_STEP_BLOCKS, tq_all // SWA_BLOCK)
    tq = nblk * SWA_BLOCK
    in_specs = [pl.BlockSpec((1, tq, dq), lambda i, j: (i, j, 0))]
    args = [q]
    if latent:
        s_len = k_lat.shape[1]
        in_specs += [pl.BlockSpec((1, s_len, dkv), lambda i, j: (i, 0, 0))] * 2
        args += [k_lat, v_lat]
    else:
        s_len = 0
    in_specs += [pl.BlockSpec((1, n_ctx, dkv), lambda i, j: (i, 0, 0))] * 2
    in_specs += [_layer_spec(sink_b, l)]
    args += [k_ctx, v_ctx, sink_b]
    return pl.pallas_call(
        functools.partial(_swa_kernel, s_len=s_len, nblk=nblk),
        grid=(b, tq_all // tq),
        in_specs=in_specs,
        out_specs=pl.BlockSpec((1, tq, dq), lambda i, j: (i, j, 0)),
        out_shape=jax.ShapeDtypeStruct((b, tq_all, dq), F32),
        compiler_params=_cparams(("arbitrary", "arbitrary")),
        name="swa_lat" if latent else "swa_ctx",
    )(*args)


CONV_ROWS = 64


def _conv_rows(prev_ref, cur_ref, next_ref, w_ref, b_ref, lg_ref, lb_ref, ext_ref, sh_ref, nt):
    j = pl.program_id(1)
    tm = cur_ref.shape[1]
    hl = CONV_HALO
    ext_ref[0:hl] = jnp.where(j > 0, prev_ref[0], 0.0)
    ext_ref[hl:hl + tm] = cur_ref[0]
    ext_ref[hl + tm:2 * hl + tm] = jnp.where(j < nt - 1, next_ref[0], 0.0)
    n_sh = sh_ref.shape[1]
    for sb in range(8):
        sh_ref[sb] = ext_ref[sb:sb + n_sh, :]
    off = hl - CONV_K // 2
    for r0 in range(0, tm, CONV_ROWS):
        acc = None
        for k in range(CONV_K):
            sb, a8 = (off + k) % 8, (off + k) // 8 * 8
            tap = sh_ref[sb, r0 + a8:r0 + a8 + CONV_ROWS, :].reshape(CONV_ROWS // 8, 8, -1)
            term = (tap * w_ref[k * 8:(k + 1) * 8, :][None]).reshape(CONV_ROWS, -1)
            acc = term if acc is None else acc + term
        hcv = acc + b_ref[...]
        mu = jnp.mean(hcv, axis=-1, keepdims=True)
        xc = hcv - mu
        y = xc * lax.rsqrt(jnp.mean(xc * xc, axis=-1, keepdims=True) + EPS) * lg_ref[...] + lb_ref[...]
        yield r0, y * jax.nn.sigmoid(y)


def _merge_kernel(oa_ref, ob_ref, gp_ref, gc_ref, gn_ref, cw_ref, cb_ref, lg_ref, lb_ref,
                  on_ref, wout_ref, x_ref, mod_ref, gm_ref, x1_ref, h2_ref, ext_ref, sh_ref, yc_ref, *, nt):
    on = on_ref[...]
    m = mod_ref[...]
    na = oa_ref.shape[2]
    nb = ob_ref.shape[2]
    tm = x_ref.shape[1]
    half = min(MERGE_HALF, tm)
    conv = _conv_rows(gp_ref, gc_ref, gn_ref, cw_ref, cb_ref, lg_ref, lb_ref, ext_ref, sh_ref, nt)
    for h0 in range(0, tm, half):
        rows = slice(h0, h0 + half)
        for _ in range(half // CONV_ROWS):
            r0, oc = next(conv)
            yc_ref[r0:r0 + CONV_ROWS, :] = _rms(oc, on[:, na + nb:]).astype(BF16)
        yab = jnp.concatenate([_rms(oa_ref[0, rows, :], on[:, 0:na]), _rms(ob_ref[0, rows, :], on[:, na:na + nb])],
                              axis=1).astype(BF16)
        proj = _dot(yab, wout_ref[0:na + nb, :]) + _dot(yc_ref[rows, :], wout_ref[na + nb:, :])
        x1 = x_ref[0, rows, :] + m[2:3] * proj
        x1_ref[0, rows, :] = x1
        h2_ref[0, rows, :] = (_rms(x1, gm_ref[...]) * (1.0 + m[4:5]) + m[3:4]).astype(BF16)


def _merge(oa, ob, glu, x, mod, mod_row, params, l):
    b, t, d = x.shape
    ch = glu.shape[2]
    tm = min(MERGE_ROWS, t)
    nt = t // tm
    hb = tm // CONV_HALO
    row = lambda i, j: (i, j, 0)
    stacks = [params[k] for k in ("conv_w", "conv_b", "conv_ln_g", "conv_ln_b", "out_norm", "w_out")]
    g_mlp = params["norm_mlp"]
    return pl.pallas_call(
        functools.partial(_merge_kernel, nt=nt),
        grid=(b, nt),
        in_specs=[
            pl.BlockSpec((1, tm, oa.shape[2]), row),
            pl.BlockSpec((1, tm, ob.shape[2]), row),
            pl.BlockSpec((1, CONV_HALO, ch), lambda i, j: (i, jnp.maximum(j * hb - 1, 0), 0)),
            pl.BlockSpec((1, tm, ch), row),
            pl.BlockSpec((1, CONV_HALO, ch), lambda i, j: (i, jnp.minimum((j + 1) * hb, nt * hb - 1), 0)),
        ] + [_layer_spec(a, l) for a in stacks] + [
            pl.BlockSpec((1, tm, d), row),
            _mod_spec(mod, l, mod_row),
            _layer_spec(g_mlp, l),
        ],
        out_specs=(pl.BlockSpec((1, tm, d), row), pl.BlockSpec((1, tm, d), row)),
        out_shape=(jax.ShapeDtypeStruct((b, t, d), F32), jax.ShapeDtypeStruct((b, t, d), BF16)),
        scratch_shapes=[pltpu.VMEM((tm + 2 * CONV_HALO, ch), F32),
                        pltpu.VMEM((8, tm + 2 * CONV_HALO - 8, ch), F32),
                        pltpu.VMEM((tm, ch), BF16)],
        compiler_params=_cparams(("arbitrary", "arbitrary")),
        name="merge",
    )(oa, ob, glu, glu, glu, *stacks, x, mod, g_mlp)


def _mlp_kernel(*refs, nf, final):
    if final:
        h_ref, w1_ref, w2_ref, x_ref, mod_ref, fn_ref, o_ref, acc_ref = refs
    else:
        h_ref, w1_ref, w2_ref, x_ref, mod_ref, o_ref, acc_ref = refs
    j = pl.program_id(2)

    @pl.when(j == 0)
    def _():
        acc_ref[...] = jnp.zeros_like(acc_ref)

    a = jnp.square(jnp.maximum(_dot(h_ref[0], w1_ref[...]), 0.0))
    acc_ref[...] += _dot(a.astype(BF16), w2_ref[...])

    @pl.when(j == nf - 1)
    def _():
        out = x_ref[0] + mod_ref[5:6, :] * acc_ref[...]
        if final:
            out = _rms(out, fn_ref[...])
        o_ref[0] = out


def _mlp(h2, x1, mod, mod_row, params, l, final_norm=None):
    b, t, d = x1.shape
    w1, w2 = params["w1"], params["w2"]
    dff = w1.shape[2]
    tr = min(MLP_ROWS, t)
    tf = MLP_FF_TILE
    nf = dff // tf
    final = final_norm is not None
    row = lambda i, r, j: (i, r, 0)
    in_specs = [
        pl.BlockSpec((1, tr, d), row),
        pl.BlockSpec((None, d, tf), lambda i, r, j: (l, 0, j)),
        pl.BlockSpec((None, tf, d), lambda i, r, j: (l, j, 0)),
        pl.BlockSpec((1, tr, d), row),
        _mod_spec(mod, l, mod_row),
    ]
    args = [h2, w1, w2, x1, mod]
    if final:
        in_specs.append(pl.BlockSpec(final_norm.shape, lambda i, r, j: (0, 0)))
        args.append(final_norm)
    return pl.pallas_call(
        functools.partial(_mlp_kernel, nf=nf, final=final),
        grid=(b, t // tr, nf),
        in_specs=in_specs,
        out_specs=pl.BlockSpec((1, tr, d), row),
        out_shape=jax.ShapeDtypeStruct((b, t, d), F32),
        scratch_shapes=[pltpu.VMEM((tr, d), F32)],
        compiler_params=_cparams(("arbitrary", "arbitrary", "arbitrary")),
        name="mlp_final" if final else "mlp",
    )(*args)


def _rope_tables(n_tok):
    rows = n_tok // GRID_W
    row = jnp.repeat(jnp.arange(rows, dtype=F32), GRID_W)
    col = jnp.tile(jnp.arange(GRID_W, dtype=F32), rows)
    n_freq = MLA_ROPE // 4
    inv_freq = ROPE_THETA ** (-jnp.arange(n_freq, dtype=F32) / n_freq)
    ar = row[:, None] * inv_freq
    ac = col[:, None] * inv_freq
    cos = jnp.concatenate([jnp.cos(ar), jnp.cos(ar), jnp.cos(ac), jnp.cos(ac)], axis=1)
    sin = jnp.concatenate([-jnp.sin(ar), jnp.sin(ar), -jnp.sin(ac), jnp.sin(ac)], axis=1)
    return jnp.tile(cos, (1, 2)), jnp.tile(sin, (1, 2))


def kernel(x, c, ctx, c_ctx, ada_w, ada_b, norm_mix, norm_mlp, w_in, mla_q_norm, mla_w_uq, mla_kv_norm, mla_w_ukv, swa_sink, conv_w, conv_b, conv_ln_g, conv_ln_b, out_norm, w_out, mlp_w1, mlp_w2, final_norm):
    b, s, d = x.shape
    n_ctx = ctx.shape[1]
    depth = ada_w.shape[0]
    assert s % (2 * MLA_Q_TILE) == 0 and s % MLP_ROWS == 0 and n_ctx % ROW_TILE == 0
    assert s % (SWA_STEP_BLOCKS * SWA_BLOCK) == 0
    assert s >= 3 * SWA_BLOCK and b + 1 <= 8

    cvec = jnp.concatenate([c, c_ctx[None, :], jnp.zeros((8 - b - 1, d), F32)], axis=0)
    mod = _ada(cvec, ada_w, ada_b).reshape(depth, 8, 6, d)
    rope_tabs = _rope_tables(s)

    vec = lambda v: v.reshape(depth, 1, -1)
    params = {
        "norm_mix": vec(norm_mix),
        "w_in": jnp.pad(w_in.astype(BF16), ((0, 0), (0, 0), (0, IN_PAD_WIDTH - w_in.shape[2]))),
        "q_norm": vec(mla_q_norm),
        "w_uq": jnp.pad(mla_w_uq.astype(BF16), ((0, 0), (0, 0), (0, 0), (0, MLA_QK_PAD - MLA_NOPE - MLA_ROPE))
                        ).reshape(depth, MLA_RANK, MLA_HEADS * MLA_QK_PAD),
        "kv_norm": vec(mla_kv_norm),
        "w_ukv": mla_w_ukv.astype(BF16).reshape(depth, MLA_RANK, MLA_HEADS * (MLA_NOPE + MLA_V)),
        "sink": jnp.broadcast_to(swa_sink[:, :, None], (depth, SWA_HEADS, 128)),
        "conv_w": jnp.broadcast_to(conv_w.reshape(depth, CONV_K, 1, CONV_CH), (depth, CONV_K, 8, CONV_CH)
                                   ).reshape(depth, CONV_K * 8, CONV_CH),
        "conv_b": vec(conv_b),
        "conv_ln_g": vec(conv_ln_g),
        "conv_ln_b": vec(conv_ln_b),
        "out_norm": vec(out_norm),
        "w_out": w_out.astype(BF16),
        "norm_mlp": vec(norm_mlp),
        "w1": mlp_w1.astype(BF16),
        "w2": mlp_w2.astype(BF16),
    }

    xc = ctx
    for l in range(depth):
        update_ctx = l < depth - 1
        qa, ka, va, bq, bk, bv, glu = _premix(x, mod, None, params, l, rope_tabs)
        qa_c, ka_c, va_c, bq_c, bk_c, bv_c, glu_c = _premix(xc, mod, b, params, l, None)

        oa = _mla(qa, [(ka, va), (ka_c, va_c)])
        ob = _swa(bq, bk, bv, bk_c, bv_c, params, l)
        x1, h2 = _merge(oa, ob, glu, x, mod, None, params, l)
        x = _mlp(h2, x1, mod, None, params, l, None if update_ctx else final_norm.reshape(1, d))

        if update_ctx:
            oa_c = _mla(qa_c, [(ka_c, va_c)])
            ob_c = _swa(bq_c, None, None, bk_c, bv_c, params, l)
            xc1, h2c = _merge(oa_c, ob_c, glu_c, xc, mod, b, params, l)
            flat = lambda a: a.reshape(1, b * n_ctx, d)
            xc = _mlp(flat(h2c), flat(xc1), mod, b, params, l).reshape(b, n_ctx, d)
    return x
```

```python
import functools

import jax
import jax.numpy as jnp
from jax import lax
from jax.experimental import pallas as pl
from jax.experimental.pallas import tpu as pltpu

F32 = jnp.float32
BF16 = jnp.bfloat16

EPS = 1e-6
NEG_INF = -1e30
GRID_W = 64
ROPE_THETA = 10000.0

MLA_HEADS = 8
MLA_RANK = 512
MLA_NOPE = 128
MLA_ROPE = 64
MLA_V = 128
MLA_QK_PAD = 256
LOG2E = 1.4426950408889634
MLA_Q_SCALE = (MLA_NOPE + MLA_ROPE) ** -0.5 * LOG2E
SWA_HEADS = 8
SWA_KV_HEADS = 2
SWA_GROUP = SWA_HEADS // SWA_KV_HEADS
SWA_DH = 64
SWA_WINDOW = 128
SWA_BLOCK = 128
SWA_Q_SCALE = SWA_DH ** -0.5 * LOG2E
CONV_CH = 512
CONV_K = 31
CONV_HALO = 16

OFF_AQ = 0
OFF_AKV = 512
OFF_KR = 1024
IN_PAD_WIDTH = 2944
TAIL_BQ = 0
TAIL_BK = 512
TAIL_BV = 640
TAIL_C = 768

ROW_TILE = 256
MERGE_ROWS = 512
MERGE_HALF = 256
PREMIX_ROWS = 512
MLP_ROWS = 512
MLP_FF_TILE = 1024
MLA_Q_TILE = 256
MLA_GROUP = 4
MLA_STEP_ROWS = 2048
MLA_SM_ROWS = 16
SWA_STEP_BLOCKS = 8
ADA_N_TILE = 1024
VMEM_LIMIT = 56 * 1024 * 1024


def _cparams(sem):
    return pltpu.CompilerParams(dimension_semantics=sem, vmem_limit_bytes=VMEM_LIMIT)


def _layer_spec(stack, l):
    _, a, b = stack.shape
    return pl.BlockSpec((None, a, b), lambda *_: (l, 0, 0), pipeline_mode=pl.Buffered(1))


def _mod_spec(mod, l, row):
    d = mod.shape[-1]
    return pl.BlockSpec((None, None, 6, d), lambda i, *_: (l, i if row is None else row, 0, 0))


def _rms(x, g):
    return x * lax.rsqrt(jnp.mean(x * x, axis=-1, keepdims=True) + EPS) * g


def _dot(a, b):
    return jnp.dot(a, b, preferred_element_type=F32)


def _dot_nt(a, b):
    return lax.dot_general(a, b, (((1,), (1,)), ((), ())), preferred_element_type=F32)


def _ada_kernel(c_ref, w_ref, b_ref, o_ref):
    c = c_ref[...]
    s = c * jax.nn.sigmoid(c)
    o_ref[0] = _dot(s.astype(BF16), w_ref[0].astype(BF16)) + b_ref[0]


def _ada(cvec, ada_w, ada_b):
    n_layers, d, n = ada_w.shape
    rows = cvec.shape[0]
    tn = ADA_N_TILE
    return pl.pallas_call(
        _ada_kernel,
        grid=(n_layers, n // tn),
        in_specs=[
            pl.BlockSpec((rows, d), lambda l, j: (0, 0)),
            pl.BlockSpec((1, d, tn), lambda l, j: (l, 0, j)),
            pl.BlockSpec((1, 1, tn), lambda l, j: (l, 0, j)),
        ],
        out_specs=pl.BlockSpec((1, rows, tn), lambda l, j: (l, 0, j)),
        out_shape=jax.ShapeDtypeStruct((n_layers, rows, n), F32),
        compiler_params=_cparams(("arbitrary", "arbitrary")),
        name="ada",
    )(cvec, ada_w, ada_b.reshape(n_layers, 1, n))


def _rope128(t, cos, sin):
    lane = lax.broadcasted_iota(jnp.int32, t.shape, 1)
    up = pltpu.roll(t, 128 - 16, 1)
    dn = pltpu.roll(t, 16, 1)
    sw = jnp.where((lane & 16) == 0, up, dn)
    return t * cos + sw * sin


def _premix_kernel(*refs, use_rope):
    if use_rope:
        (x_ref, mod_ref, g_ref, win_ref, qn_ref, wuq_ref, kvn_ref, wukv_ref, cos_ref, sin_ref,
         qa_ref, ka_ref, va_ref, bq_ref, bk_ref, bv_ref, glu_ref) = refs
        cos = cos_ref[...]
        sin = sin_ref[...]
        rope = lambda t: _rope128(t, cos, sin)
    else:
        (x_ref, mod_ref, g_ref, win_ref, qn_ref, wuq_ref, kvn_ref, wukv_ref,
         qa_ref, ka_ref, va_ref, bq_ref, bk_ref, bv_ref, glu_ref) = refs
        rope = lambda t: t

    m = mod_ref[...]
    h = _rms(x_ref[0], g_ref[...]) * (1.0 + m[1:2]) + m[0:1]
    p = _dot(h.astype(BF16), win_ref[...])
    tail = pltpu.roll(p[:, OFF_KR:], IN_PAD_WIDTH - OFF_KR - MLA_ROPE, 1)
    lane = lax.broadcasted_iota(jnp.int32, (p.shape[0], 128), 1)
    kr = jnp.where(lane < MLA_ROPE, p[:, OFF_KR:OFF_KR + 128], 0.0)

    qn = _rms(p[:, OFF_AQ:OFF_AQ + MLA_RANK], qn_ref[...])
    q = _dot(qn.astype(BF16), wuq_ref[...])
    for hd in range(MLA_HEADS):
        c0 = hd * MLA_QK_PAD
        qa_ref[0, hd, :, 0:128] = (q[:, c0:c0 + 128] * MLA_Q_SCALE).astype(BF16)
        qa_ref[0, hd, :, 128:256] = (rope(q[:, c0 + 128:c0 + 256]) * MLA_Q_SCALE).astype(BF16)

    kvn = _rms(p[:, OFF_AKV:OFF_AKV + MLA_RANK], kvn_ref[...])
    kv = _dot(kvn.astype(BF16), wukv_ref[...])
    kpe = rope(kr).astype(BF16)
    for hd in range(MLA_HEADS):
        c0 = hd * (MLA_NOPE + MLA_V)
        ka_ref[0, hd, :, 0:128] = kv[:, c0:c0 + 128].astype(BF16)
        ka_ref[0, hd, :, 128:256] = kpe
        va_ref[0, hd] = kv[:, c0 + 128:c0 + 256].astype(BF16)

    for t in range(SWA_HEADS * SWA_DH // 128):
        c0 = TAIL_BQ + t * 128
        bq_ref[0, :, t * 128:(t + 1) * 128] = (rope(tail[:, c0:c0 + 128]) * SWA_Q_SCALE).astype(BF16)
    bk_ref[0] = rope(tail[:, TAIL_BK:TAIL_BK + 128]).astype(BF16)
    bv_ref[0] = tail[:, TAIL_BV:TAIL_BV + 128].astype(BF16)

    glu_ref[0] = (tail[:, TAIL_C:TAIL_C + CONV_CH]
                  * jax.nn.sigmoid(tail[:, TAIL_C + CONV_CH:TAIL_C + 2 * CONV_CH]))


def _premix(x, mod, mod_row, params, l, rope_tabs):
    b, t, d = x.shape
    tm = min(PREMIX_ROWS, t)
    use_rope = rope_tabs is not None
    stacks = [params[k] for k in ("norm_mix", "w_in", "q_norm", "w_uq", "kv_norm", "w_ukv")]
    in_specs = [pl.BlockSpec((1, tm, d), lambda i, j: (i, j, 0)), _mod_spec(mod, l, mod_row)]
    in_specs += [_layer_spec(a, l) for a in stacks]
    args = [x, mod] + stacks
    if use_rope:
        in_specs += [pl.BlockSpec((tm, 128), lambda i, j: (j, 0))] * 2
        args += list(rope_tabs)
    hq = MLA_HEADS
    out_shape = (
        jax.ShapeDtypeStruct((b, hq, t, MLA_QK_PAD), BF16),
        jax.ShapeDtypeStruct((b, hq, t, MLA_QK_PAD), BF16),
        jax.ShapeDtypeStruct((b, hq, t, MLA_V), BF16),
        jax.ShapeDtypeStruct((b, t, SWA_HEADS * SWA_DH), BF16),
        jax.ShapeDtypeStruct((b, t, SWA_KV_HEADS * SWA_DH), BF16),
        jax.ShapeDtypeStruct((b, t, SWA_KV_HEADS * SWA_DH), BF16),
        jax.ShapeDtypeStruct((b, t, CONV_CH), F32),
    )
    out_specs = (
        pl.BlockSpec((1, hq, tm, MLA_QK_PAD), lambda i, j: (i, 0, j, 0)),
        pl.BlockSpec((1, hq, tm, MLA_QK_PAD), lambda i, j: (i, 0, j, 0)),
        pl.BlockSpec((1, hq, tm, MLA_V), lambda i, j: (i, 0, j, 0)),
        pl.BlockSpec((1, tm, SWA_HEADS * SWA_DH), lambda i, j: (i, j, 0)),
        pl.BlockSpec((1, tm, SWA_KV_HEADS * SWA_DH), lambda i, j: (i, j, 0)),
        pl.BlockSpec((1, tm, SWA_KV_HEADS * SWA_DH), lambda i, j: (i, j, 0)),
        pl.BlockSpec((1, tm, CONV_CH), lambda i, j: (i, j, 0)),
    )
    return pl.pallas_call(
        functools.partial(_premix_kernel, use_rope=use_rope),
        grid=(b, t // tm),
        in_specs=in_specs,
        out_specs=out_specs,
        out_shape=out_shape,
        compiler_params=_cparams(("arbitrary", "arbitrary")),
        name="premix_rope" if use_rope else "premix",
    )(*args)


def _mla_kernel(*refs, src_lens, tile):
    n_src = len(src_lens)
    q_ref = refs[0]
    kv_refs = refs[1:1 + 2 * n_src]
    o_ref = refs[1 + 2 * n_src]
    scratch = refs[2 + 2 * n_src:]
    group = len(scratch) // 3
    s_refs, p_refs, l_refs = scratch[:group], scratch[group:2 * group], scratch[2 * group:]
    offs = [sum(src_lens[:i]) for i in range(n_src)]
    n_groups = q_ref.shape[2] // (group * tile)

    def one_group(gi):
        def rows(t):
            start = (group * gi + t) * tile
            return pl.ds(start if isinstance(gi, int) else pl.multiple_of(start, tile), tile)

        _mla_group(q_ref, kv_refs, o_ref, s_refs, p_refs, l_refs, rows, src_lens, offs, tile)

    if n_groups == 1:
        one_group(0)
    else:
        pl.loop(0, n_groups)(one_group)


def _mla_group(q_ref, kv_refs, o_ref, s_refs, p_refs, l_refs, rows, src_lens, offs, tile):
    n_src = len(src_lens)
    group = len(s_refs)

    def scores(t):
        qt = q_ref[0, 0, rows(t), :]
        for si in range(n_src):
            s_refs[t][:, offs[si]:offs[si] + src_lens[si]] = _dot_nt(qt, kv_refs[2 * si][0, 0])

    def softmax(t):
        for r in range(0, tile, MLA_SM_ROWS):
            rs = slice(r, r + MLA_SM_ROWS)
            sb = s_refs[t][rs, :]
            p = jnp.exp2(sb - jnp.max(sb, axis=1, keepdims=True))
            l_refs[t][rs, :] = jnp.broadcast_to(jnp.sum(p, axis=1, keepdims=True), (MLA_SM_ROWS, 128))
            p_refs[t][rs, :] = p.astype(BF16)

    def values(t):
        acc = None
        for si in range(n_src):
            part = _dot(p_refs[t][:, offs[si]:offs[si] + src_lens[si]], kv_refs[2 * si + 1][0, 0])
            acc = part if acc is None else acc + part
        o_ref[0, rows(t), :] = acc / l_refs[t][...]

    scores(0)
    for t in range(group):
        if t + 1 < group:
            scores(t + 1)
        softmax(t)
        values(t)


def _mla(q, kv_sources):
    b, hq, tq_all, dq = q.shape
    tile = min(MLA_Q_TILE, tq_all // 2)
    tq = min(MLA_STEP_ROWS, tq_all)
    group = min(MLA_GROUP, tq // tile)
    assert tq_all % tq == 0 and tq % (group * tile) == 0
    in_specs = [pl.BlockSpec((1, 1, tq, dq), lambda i, h, j: (i, h, j, 0))]
    args = [q]
    src_lens = []
    for k, v in kv_sources:
        n = k.shape[2]
        src_lens.append(n)
        in_specs.append(pl.BlockSpec((1, 1, n, dq), lambda i, h, j: (i, h, 0, 0)))
        in_specs.append(pl.BlockSpec((1, 1, n, MLA_V), lambda i, h, j: (i, h, 0, 0)))
        args += [k, v]
    n_keys = sum(src_lens)
    return pl.pallas_call(
        functools.partial(_mla_kernel, src_lens=tuple(src_lens), tile=tile),
        grid=(b, hq, tq_all // tq),
        in_specs=in_specs,
        out_specs=pl.BlockSpec((1, tq, MLA_V), lambda i, h, j: (i, j, h)),
        out_shape=jax.ShapeDtypeStruct((b, tq_all, hq * MLA_V), F32),
        scratch_shapes=(
            [pltpu.VMEM((tile, n_keys), F32)] * group
            + [pltpu.VMEM((tile, n_keys), BF16)] * group
            + [pltpu.VMEM((tile, 128), F32)] * group),
        compiler_params=_cparams(("arbitrary", "arbitrary", "arbitrary")),
        name="mla_%d" % len(kv_sources),
    )(*args)


def _swa_kernel(*refs, s_len, nblk):
    latent = s_len > 0
    if latent:
        q_ref, kl_ref, vl_ref, kc_ref, vc_ref, sink_ref, o_ref = refs
    else:
        q_ref, kc_ref, vc_ref, sink_ref, o_ref = refs
    kc = kc_ref[0]
    vc = vc_ref[0]
    rows = SWA_GROUP * SWA_BLOCK
    win = 3 * SWA_BLOCK
    chains = [(blk, kh) for blk in range(nblk) for kh in range(SWA_KV_HEADS)]
    kcat, vcat, valid = {}, {}, {}
    n_win_tiles = win // 128 if latent else 0
    if latent:
        d = (lax.broadcasted_iota(jnp.int32, (rows, win), 1)
             - (lax.broadcasted_iota(jnp.int32, (rows, win), 0) & (SWA_BLOCK - 1)))
        for blk in range(nblk):
            n = pl.program_id(1) * nblk + blk
            start = jnp.clip((n - 1) * SWA_BLOCK, 0, s_len - win)
            start = pl.multiple_of(start, SWA_BLOCK)
            kcat[blk] = jnp.concatenate([kl_ref[0, pl.ds(start, win), :], kc], axis=0)
            vcat[blk] = jnp.concatenate([vl_ref[0, pl.ds(start, win), :], vc], axis=0)
            valid[blk] = jnp.abs(d + (start - n * SWA_BLOCK)) <= SWA_WINDOW
    else:
        for blk in range(nblk):
            kcat[blk], vcat[blk] = kc, vc

    s, snk, mx = {}, {}, {}
    for c in chains:
        blk, kh = c
        heads = range(kh * SWA_GROUP, (kh + 1) * SWA_GROUP)
        lo, hi = kh * SWA_DH, (kh + 1) * SWA_DH
        q = q_ref[0, blk * SWA_BLOCK:(blk + 1) * SWA_BLOCK, :]
        qs = jnp.concatenate([q[:, h * SWA_DH:(h + 1) * SWA_DH] for h in heads], axis=0)
        snk[c] = jnp.concatenate(
            [jnp.broadcast_to(sink_ref[h:h + 1, 0:1] * LOG2E, (SWA_BLOCK, 1)) for h in heads], axis=0)
        sc = _dot_nt(qs, kcat[blk][:, lo:hi])
        tiles = [sc[:, j * 128:(j + 1) * 128] for j in range(sc.shape[1] // 128)]
        for j in range(n_win_tiles):
            tiles[j] = jnp.where(valid[blk][:, j * 128:(j + 1) * 128], tiles[j], NEG_INF)
        s[c] = tiles
    for c in chains:
        mm = functools.reduce(jnp.maximum, s[c])
        mx[c] = jnp.maximum(jnp.max(mm, axis=1, keepdims=True), snk[c])
    prob, den = {}, {}
    for c in chains:
        ps = [jnp.exp2(t - mx[c]) for t in s[c]]
        den[c] = jnp.sum(functools.reduce(jnp.add, ps), axis=1, keepdims=True) + jnp.exp2(snk[c] - mx[c])
        prob[c] = jnp.concatenate([p.astype(BF16) for p in ps], axis=1)
    for c in chains:
        blk, kh = c
        lo, hi = kh * SWA_DH, (kh + 1) * SWA_DH
        o = _dot(prob[c], vcat[blk][:, lo:hi]) / den[c]
        for g in range(SWA_GROUP):
            h = kh * SWA_GROUP + g
            o_ref[0, blk * SWA_BLOCK:(blk + 1) * SWA_BLOCK, h * SWA_DH:(h + 1) * SWA_DH] = (
                o[g * SWA_BLOCK:(g + 1) * SWA_BLOCK])


def _swa(q, k_lat, v_lat, k_ctx, v_ctx, params, l):
    sink_b = params["sink"]
    b, tq_all, dq = q.shape
    dkv = k_ctx.shape[2]
    n_ctx = k_ctx.shape[1]
    latent = k_lat is not None
    nblk = min(SWA_STEP_BLOCKS, tq_all // SWA_BLOCK)
    tq = nblk * SWA_BLOCK
    in_specs = [pl.BlockSpec((1, tq, dq), lambda i, j: (i, j, 0))]
    args = [q]
    if latent:
        s_len = k_lat.shape[1]
        in_specs += [pl.BlockSpec((1, s_len, dkv), lambda i, j: (i, 0, 0))] * 2
        args += [k_lat, v_lat]
    else:
        s_len = 0
    in_specs += [pl.BlockSpec((1, n_ctx, dkv), lambda i, j: (i, 0, 0))] * 2
    in_specs += [_layer_spec(sink_b, l)]
    args += [k_ctx, v_ctx, sink_b]
    return pl.pallas_call(
        functools.partial(_swa_kernel, s_len=s_len, nblk=nblk),
        grid=(b, tq_all // tq),
        in_specs=in_specs,
        out_specs=pl.BlockSpec((1, tq, dq), lambda i, j: (i, j, 0)),
        out_shape=jax.ShapeDtypeStruct((b, tq_all, dq), F32),
        compiler_params=_cparams(("arbitrary", "arbitrary")),
        name="swa_lat" if latent else "swa_ctx",
    )(*args)


CONV_ROWS = 64


def _conv_rows(prev_ref, cur_ref, next_ref, w_ref, b_ref, lg_ref, lb_ref, ext_ref, sh_ref, nt):
    j = pl.program_id(1)
    tm = cur_ref.shape[1]
    hl = CONV_HALO
    ext_ref[0:hl] = jnp.where(j > 0, prev_ref[0], 0.0)
    ext_ref[hl:hl + tm] = cur_ref[0]
    ext_ref[hl + tm:2 * hl + tm] = jnp.where(j < nt - 1, next_ref[0], 0.0)
    n_sh = sh_ref.shape[1]
    for sb in range(8):
        sh_ref[sb] = ext_ref[sb:sb + n_sh, :]
    off = hl - CONV_K // 2
    for r0 in range(0, tm, CONV_ROWS):
        acc = None
        for k in range(CONV_K):
            sb, a8 = (off + k) % 8, (off + k) // 8 * 8
            tap = sh_ref[sb, r0 + a8:r0 + a8 + CONV_ROWS, :].reshape(CONV_ROWS // 8, 8, -1)
            term = (tap * w_ref[k * 8:(k + 1) * 8, :][None]).reshape(CONV_ROWS, -1)
            acc = term if acc is None else acc + term
        hcv = acc + b_ref[...]
        mu = jnp.mean(hcv, axis=-1, keepdims=True)
        xc = hcv - mu
        y = xc * lax.rsqrt(jnp.mean(xc * xc, axis=-1, keepdims=True) + EPS) * lg_ref[...] + lb_ref[...]
        yield r0, y * jax.nn.sigmoid(y)


def _merge_kernel(oa_ref, ob_ref, gp_ref, gc_ref, gn_ref, cw_ref, cb_ref, lg_ref, lb_ref,
                  on_ref, wout_ref, x_ref, mod_ref, gm_ref, x1_ref, h2_ref, ext_ref, sh_ref, yc_ref, *, nt):
    on = on_ref[...]
    m = mod_ref[...]
    na = oa_ref.shape[2]
    nb = ob_ref.shape[2]
    tm = x_ref.shape[1]
    half = min(MERGE_HALF, tm)
    conv = _conv_rows(gp_ref, gc_ref, gn_ref, cw_ref, cb_ref, lg_ref, lb_ref, ext_ref, sh_ref, nt)
    for h0 in range(0, tm, half):
        rows = slice(h0, h0 + half)
        for _ in range(half // CONV_ROWS):
            r0, oc = next(conv)
            yc_ref[r0:r0 + CONV_ROWS, :] = _rms(oc, on[:, na + nb:]).astype(BF16)
        yab = jnp.concatenate([_rms(oa_ref[0, rows, :], on[:, 0:na]), _rms(ob_ref[0, rows, :], on[:, na:na + nb])],
                              axis=1).astype(BF16)
        proj = _dot(yab, wout_ref[0:na + nb, :]) + _dot(yc_ref[rows, :], wout_ref[na + nb:, :])
        x1 = x_ref[0, rows, :] + m[2:3] * proj
        x1_ref[0, rows, :] = x1
        h2_ref[0, rows, :] = (_rms(x1, gm_ref[...]) * (1.0 + m[4:5]) + m[3:4]).astype(BF16)


def _merge(oa, ob, glu, x, mod, mod_row, params, l):
    b, t, d = x.shape
    ch = glu.shape[2]
    tm = min(MERGE_ROWS, t)
    nt = t // tm
    hb = tm // CONV_HALO
    row = lambda i, j: (i, j, 0)
    stacks = [params[k] for k in ("conv_w", "conv_b", "conv_ln_g", "conv_ln_b", "out_norm", "w_out")]
    g_mlp = params["norm_mlp"]
    return pl.pallas_call(
        functools.partial(_merge_kernel, nt=nt),
        grid=(b, nt),
        in_specs=[
            pl.BlockSpec((1, tm, oa.shape[2]), row),
            pl.BlockSpec((1, tm, ob.shape[2]), row),
            pl.BlockSpec((1, CONV_HALO, ch), lambda i, j: (i, jnp.maximum(j * hb - 1, 0), 0)),
            pl.BlockSpec((1, tm, ch), row),
            pl.BlockSpec((1, CONV_HALO, ch), lambda i, j: (i, jnp.minimum((j + 1) * hb, nt * hb - 1), 0)),
        ] + [_layer_spec(a, l) for a in stacks] + [
            pl.BlockSpec((1, tm, d), row),
            _mod_spec(mod, l, mod_row),
            _layer_spec(g_mlp, l),
        ],
        out_specs=(pl.BlockSpec((1, tm, d), row), pl.BlockSpec((1, tm, d), row)),
        out_shape=(jax.ShapeDtypeStruct((b, t, d), F32), jax.ShapeDtypeStruct((b, t, d), BF16)),
        scratch_shapes=[pltpu.VMEM((tm + 2 * CONV_HALO, ch), F32),
                        pltpu.VMEM((8, tm + 2 * CONV_HALO - 8, ch), F32),
                        pltpu.VMEM((tm, ch), BF16)],
        compiler_params=_cparams(("arbitrary", "arbitrary")),
        name="merge",
    )(oa, ob, glu, glu, glu, *stacks, x, mod, g_mlp)


def _mlp_kernel(*refs, nf, final):
    if final:
        h_ref, w1_ref, w2_ref, x_ref, mod_ref, fn_ref, o_ref, acc_ref = refs
    else:
        h_ref, w1_ref, w2_ref, x_ref, mod_ref, o_ref, acc_ref = refs
    j = pl.program_id(2)

    @pl.when(j == 0)
    def _():
        acc_ref[...] = jnp.zeros_like(acc_ref)

    a = jnp.square(jnp.maximum(_dot(h_ref[0], w1_ref[...]), 0.0))
    acc_ref[...] += _dot(a.astype(BF16), w2_ref[...])

    @pl.when(j == nf - 1)
    def _():
        out = x_ref[0] + mod_ref[5:6, :] * acc_ref[...]
        if final:
            out = _rms(out, fn_ref[...])
        o_ref[0] = out


def _mlp(h2, x1, mod, mod_row, params, l, final_norm=None):
    b, t, d = x1.shape
    w1, w2 = params["w1"], params["w2"]
    tr = min(MLP_ROWS, t)
    nf, tf = w1.shape[1], w1.shape[3]
    final = final_norm is not None
    row = lambda i, r, j: (i, r, 0)
    in_specs = [
        pl.BlockSpec((1, tr, d), row),
        pl.BlockSpec((None, None, d, tf), lambda i, r, j: (l, j, 0, 0)),
        pl.BlockSpec((None, tf, d), lambda i, r, j: (l, j, 0)),
        pl.BlockSpec((1, tr, d), row),
        _mod_spec(mod, l, mod_row),
    ]
    args = [h2, w1, w2, x1, mod]
    if final:
        in_specs.append(pl.BlockSpec(final_norm.shape, lambda i, r, j: (0, 0)))
        args.append(final_norm)
    return pl.pallas_call(
        functools.partial(_mlp_kernel, nf=nf, final=final),
        grid=(b, t // tr, nf),
        in_specs=in_specs,
        out_specs=pl.BlockSpec((1, tr, d), row),
        out_shape=jax.ShapeDtypeStruct((b, t, d), F32),
        scratch_shapes=[pltpu.VMEM((tr, d), F32)],
        compiler_params=_cparams(("arbitrary", "arbitrary", "arbitrary")),
        name="mlp_final" if final else "mlp",
    )(*args)


def _rope_tables(n_tok):
    rows = n_tok // GRID_W
    row = jnp.repeat(jnp.arange(rows, dtype=F32), GRID_W)
    col = jnp.tile(jnp.arange(GRID_W, dtype=F32), rows)
    n_freq = MLA_ROPE // 4
    inv_freq = ROPE_THETA ** (-jnp.arange(n_freq, dtype=F32) / n_freq)
    ar = row[:, None] * inv_freq
    ac = col[:, None] * inv_freq
    cos = jnp.concatenate([jnp.cos(ar), jnp.cos(ar), jnp.cos(ac), jnp.cos(ac)], axis=1)
    sin = jnp.concatenate([-jnp.sin(ar), jnp.sin(ar), -jnp.sin(ac), jnp.sin(ac)], axis=1)
    return jnp.tile(cos, (1, 2)), jnp.tile(sin, (1, 2))


def kernel(x, c, ctx, c_ctx, ada_w, ada_b, norm_mix, norm_mlp, w_in, mla_q_norm, mla_w_uq, mla_kv_norm, mla_w_ukv, swa_sink, conv_w, conv_b, conv_ln_g, conv_ln_b, out_norm, w_out, mlp_w1, mlp_w2, final_norm):
    b, s, d = x.shape
    n_ctx = ctx.shape[1]
    depth = ada_w.shape[0]
    assert s % (2 * MLA_Q_TILE) == 0 and s % MLP_ROWS == 0 and n_ctx % ROW_TILE == 0
    assert s % (SWA_STEP_BLOCKS * SWA_BLOCK) == 0
    assert s >= 3 * SWA_BLOCK and b + 1 <= 8

    cvec = jnp.concatenate([c, c_ctx[None, :], jnp.zeros((8 - b - 1, d), F32)], axis=0)
    mod = _ada(cvec, ada_w, ada_b).reshape(depth, 8, 6, d)
    rope_tabs = _rope_tables(s)

    vec = lambda v: v.reshape(depth, 1, -1)
    params = {
        "norm_mix": vec(norm_mix),
        "w_in": jnp.pad(w_in.astype(BF16), ((0, 0), (0, 0), (0, IN_PAD_WIDTH - w_in.shape[2]))),
        "q_norm": vec(mla_q_norm),
        "w_uq": jnp.pad(mla_w_uq.astype(BF16), ((0, 0), (0, 0), (0, 0), (0, MLA_QK_PAD - MLA_NOPE - MLA_ROPE))
                        ).reshape(depth, MLA_RANK, MLA_HEADS * MLA_QK_PAD),
        "kv_norm": vec(mla_kv_norm),
        "w_ukv": mla_w_ukv.astype(BF16).reshape(depth, MLA_RANK, MLA_HEADS * (MLA_NOPE + MLA_V)),
        "sink": jnp.broadcast_to(swa_sink[:, :, None], (depth, SWA_HEADS, 128)),
        "conv_w": jnp.broadcast_to(conv_w.reshape(depth, CONV_K, 1, CONV_CH), (depth, CONV_K, 8, CONV_CH)
                                   ).reshape(depth, CONV_K * 8, CONV_CH),
        "conv_b": vec(conv_b),
        "conv_ln_g": vec(conv_ln_g),
        "conv_ln_b": vec(conv_ln_b),
        "out_norm": vec(out_norm),
        "w_out": w_out.astype(BF16),
        "norm_mlp": vec(norm_mlp),
        "w1": mlp_w1.astype(BF16).reshape(depth, d, -1, MLP_FF_TILE).transpose(0, 2, 1, 3),
        "w2": mlp_w2.astype(BF16),
    }

    xc = ctx
    for l in range(depth):
        update_ctx = l < depth - 1
        qa, ka, va, bq, bk, bv, glu = _premix(x, mod, None, params, l, rope_tabs)
        qa_c, ka_c, va_c, bq_c, bk_c, bv_c, glu_c = _premix(xc, mod, b, params, l, None)

        oa = _mla(qa, [(ka, va), (ka_c, va_c)])
        ob = _swa(bq, bk, bv, bk_c, bv_c, params, l)
        x1, h2 = _merge(oa, ob, glu, x, mod, None, params, l)
        x = _mlp(h2, x1, mod, None, params, l, None if update_ctx else final_norm.reshape(1, d))

        if update_ctx:
            oa_c = _mla(qa_c, [(ka_c, va_c)])
            ob_c = _swa(bq_c, None, None, bk_c, bv_c, params, l)
            xc1, h2c = _merge(oa_c, ob_c, glu_c, xc, mod, b, params, l)
            flat = lambda a: a.reshape(1, b * n_ctx, d)
            xc = _mlp(flat(h2c), flat(xc1), mod, b, params, l).reshape(b, n_ctx, d)
    return x
```

```python
import functools

import jax
import jax.numpy as jnp
from jax import lax
from jax.experimental import pallas as pl
from jax.experimental.pallas import tpu as pltpu

F32 = jnp.float32
BF16 = jnp.bfloat16

EPS = 1e-6
NEG_INF = -1e30
GRID_W = 64
ROPE_THETA = 10000.0

MLA_HEADS = 8
MLA_RANK = 512
MLA_NOPE = 128
MLA_ROPE = 64
MLA_V = 128
MLA_QK_PAD = 256
LOG2E = 1.4426950408889634
MLA_Q_SCALE = (MLA_NOPE + MLA_ROPE) ** -0.5 * LOG2E
SWA_HEADS = 8
SWA_KV_HEADS = 2
SWA_GROUP = SWA_HEADS // SWA_KV_HEADS
SWA_DH = 64
SWA_WINDOW = 128
SWA_BLOCK = 128
SWA_Q_SCALE = SWA_DH ** -0.5 * LOG2E
CONV_CH = 512
CONV_K = 31
CONV_HALO = 16

OFF_AQ = 0
OFF_AKV = 512
OFF_KR = 1024
IN_PAD_WIDTH = 2944
TAIL_BQ = 0
TAIL_BK = 512
TAIL_BV = 640
TAIL_C = 768

ROW_TILE = 256
MERGE_ROWS = 512
MERGE_HALF = 256
PREMIX_ROWS = 512
MLP_ROWS = 512
MLP_FF_TILE = 1024
MLA_Q_TILE = 256
MLA_GROUP = 8
MLA_SCORE_SLOTS = 2
MLA_STEP_ROWS = 2048
MLA_SM_ROWS = 16
SWA_STEP_BLOCKS = 8
ADA_N_TILE = 1024
VMEM_LIMIT = 56 * 1024 * 1024


def _cparams(sem):
    return pltpu.CompilerParams(dimension_semantics=sem, vmem_limit_bytes=VMEM_LIMIT)


def _layer_spec(stack, l):
    _, a, b = stack.shape
    return pl.BlockSpec((None, a, b), lambda *_: (l, 0, 0), pipeline_mode=pl.Buffered(1))


def _mod_spec(mod, l, row):
    d = mod.shape[-1]
    return pl.BlockSpec((None, None, 6, d), lambda i, *_: (l, i if row is None else row, 0, 0))


def _rms(x, g):
    return x * lax.rsqrt(jnp.mean(x * x, axis=-1, keepdims=True) + EPS) * g


def _dot(a, b):
    return jnp.dot(a, b, preferred_element_type=F32)


def _dot_nt(a, b):
    return lax.dot_general(a, b, (((1,), (1,)), ((), ())), preferred_element_type=F32)


def _ada_kernel(c_ref, w_ref, b_ref, o_ref):
    c = c_ref[...]
    s = c * jax.nn.sigmoid(c)
    o_ref[0] = _dot(s.astype(BF16), w_ref[0].astype(BF16)) + b_ref[0]


def _ada(cvec, ada_w, ada_b):
    n_layers, d, n = ada_w.shape
    rows = cvec.shape[0]
    tn = ADA_N_TILE
    return pl.pallas_call(
        _ada_kernel,
        grid=(n_layers, n // tn),
        in_specs=[
            pl.BlockSpec((rows, d), lambda l, j: (0, 0)),
            pl.BlockSpec((1, d, tn), lambda l, j: (l, 0, j)),
            pl.BlockSpec((1, 1, tn), lambda l, j: (l, 0, j)),
        ],
        out_specs=pl.BlockSpec((1, rows, tn), lambda l, j: (l, 0, j)),
        out_shape=jax.ShapeDtypeStruct((n_layers, rows, n), F32),
        compiler_params=_cparams(("arbitrary", "arbitrary")),
        name="ada",
    )(cvec, ada_w, ada_b.reshape(n_layers, 1, n))


def _rope128(t, cos, sin):
    lane = lax.broadcasted_iota(jnp.int32, t.shape, 1)
    up = pltpu.roll(t, 128 - 16, 1)
    dn = pltpu.roll(t, 16, 1)
    sw = jnp.where((lane & 16) == 0, up, dn)
    return t * cos + sw * sin


def _premix_kernel(*refs, use_rope):
    if use_rope:
        (x_ref, mod_ref, g_ref, win_ref, qn_ref, wuq_ref, kvn_ref, wukv_ref, cos_ref, sin_ref,
         qa_ref, ka_ref, va_ref, bq_ref, bk_ref, bv_ref, glu_ref) = refs
        cos = cos_ref[...]
        sin = sin_ref[...]
        rope = lambda t: _rope128(t, cos, sin)
    else:
        (x_ref, mod_ref, g_ref, win_ref, qn_ref, wuq_ref, kvn_ref, wukv_ref,
         qa_ref, ka_ref, va_ref, bq_ref, bk_ref, bv_ref, glu_ref) = refs
        rope = lambda t: t

    m = mod_ref[...]
    h = _rms(x_ref[0], g_ref[...]) * (1.0 + m[1:2]) + m[0:1]
    p = _dot(h.astype(BF16), win_ref[...])
    tail = pltpu.roll(p[:, OFF_KR:], IN_PAD_WIDTH - OFF_KR - MLA_ROPE, 1)
    lane = lax.broadcasted_iota(jnp.int32, (p.shape[0], 128), 1)
    kr = jnp.where(lane < MLA_ROPE, p[:, OFF_KR:OFF_KR + 128], 0.0)

    qn = _rms(p[:, OFF_AQ:OFF_AQ + MLA_RANK], qn_ref[...])
    q = _dot(qn.astype(BF16), wuq_ref[...])
    for hd in range(MLA_HEADS):
        c0 = hd * MLA_QK_PAD
        qa_ref[0, hd, :, 0:128] = (q[:, c0:c0 + 128] * MLA_Q_SCALE).astype(BF16)
        qa_ref[0, hd, :, 128:256] = (rope(q[:, c0 + 128:c0 + 256]) * MLA_Q_SCALE).astype(BF16)

    kvn = _rms(p[:, OFF_AKV:OFF_AKV + MLA_RANK], kvn_ref[...])
    kv = _dot(kvn.astype(BF16), wukv_ref[...])
    kpe = rope(kr).astype(BF16)
    for hd in range(MLA_HEADS):
        c0 = hd * (MLA_NOPE + MLA_V)
        ka_ref[0, hd, :, 0:128] = kv[:, c0:c0 + 128].astype(BF16)
        ka_ref[0, hd, :, 128:256] = kpe
        va_ref[0, hd] = kv[:, c0 + 128:c0 + 256].astype(BF16)

    for t in range(SWA_HEADS * SWA_DH // 128):
        c0 = TAIL_BQ + t * 128
        bq_ref[0, :, t * 128:(t + 1) * 128] = (rope(tail[:, c0:c0 + 128]) * SWA_Q_SCALE).astype(BF16)
    bk_ref[0] = rope(tail[:, TAIL_BK:TAIL_BK + 128]).astype(BF16)
    bv_ref[0] = tail[:, TAIL_BV:TAIL_BV + 128].astype(BF16)

    glu_ref[0] = (tail[:, TAIL_C:TAIL_C + CONV_CH]
                  * jax.nn.sigmoid(tail[:, TAIL_C + CONV_CH:TAIL_C + 2 * CONV_CH]))


def _premix(x, mod, mod_row, params, l, rope_tabs):
    b, t, d = x.shape
    tm = min(PREMIX_ROWS, t)
    use_rope = rope_tabs is not None
    stacks = [params[k] for k in ("norm_mix", "w_in", "q_norm", "w_uq", "kv_norm", "w_ukv")]
    in_specs = [pl.BlockSpec((1, tm, d), lambda i, j: (i, j, 0)), _mod_spec(mod, l, mod_row)]
    in_specs += [_layer_spec(a, l) for a in stacks]
    args = [x, mod] + stacks
    if use_rope:
        in_specs += [pl.BlockSpec((tm, 128), lambda i, j: (j, 0))] * 2
        args += list(rope_tabs)
    hq = MLA_HEADS
    out_shape = (
        jax.ShapeDtypeStruct((b, hq, t, MLA_QK_PAD), BF16),
        jax.ShapeDtypeStruct((b, hq, t, MLA_QK_PAD), BF16),
        jax.ShapeDtypeStruct((b, hq, t, MLA_V), BF16),
        jax.ShapeDtypeStruct((b, t, SWA_HEADS * SWA_DH), BF16),
        jax.ShapeDtypeStruct((b, t, SWA_KV_HEADS * SWA_DH), BF16),
        jax.ShapeDtypeStruct((b, t, SWA_KV_HEADS * SWA_DH), BF16),
        jax.ShapeDtypeStruct((b, t, CONV_CH), F32),
    )
    out_specs = (
        pl.BlockSpec((1, hq, tm, MLA_QK_PAD), lambda i, j: (i, 0, j, 0)),
        pl.BlockSpec((1, hq, tm, MLA_QK_PAD), lambda i, j: (i, 0, j, 0)),
        pl.BlockSpec((1, hq, tm, MLA_V), lambda i, j: (i, 0, j, 0)),
        pl.BlockSpec((1, tm, SWA_HEADS * SWA_DH), lambda i, j: (i, j, 0)),
        pl.BlockSpec((1, tm, SWA_KV_HEADS * SWA_DH), lambda i, j: (i, j, 0)),
        pl.BlockSpec((1, tm, SWA_KV_HEADS * SWA_DH), lambda i, j: (i, j, 0)),
        pl.BlockSpec((1, tm, CONV_CH), lambda i, j: (i, j, 0)),
    )
    return pl.pallas_call(
        functools.partial(_premix_kernel, use_rope=use_rope),
        grid=(b, t // tm),
        in_specs=in_specs,
        out_specs=out_specs,
        out_shape=out_shape,
        compiler_params=_cparams(("arbitrary", "arbitrary")),
        name="premix_rope" if use_rope else "premix",
    )(*args)


def _mla_kernel(*refs, src_lens, tile, n_s):
    n_src = len(src_lens)
    q_ref = refs[0]
    kv_refs = refs[1:1 + 2 * n_src]
    o_ref = refs[1 + 2 * n_src]
    scratch = refs[2 + 2 * n_src:]
    group = (len(scratch) - n_s) // 2
    s_refs, p_refs, l_refs = scratch[:n_s], scratch[n_s:n_s + group], scratch[n_s + group:]
    offs = [sum(src_lens[:i]) for i in range(n_src)]
    n_groups = q_ref.shape[2] // (group * tile)

    def one_group(gi):
        def rows(t):
            start = (group * gi + t) * tile
            return pl.ds(start if isinstance(gi, int) else pl.multiple_of(start, tile), tile)

        _mla_group(q_ref, kv_refs, o_ref, s_refs, p_refs, l_refs, rows, src_lens, offs, tile)

    if n_groups == 1:
        one_group(0)
    else:
        pl.loop(0, n_groups)(one_group)


def _mla_group(q_ref, kv_refs, o_ref, s_refs, p_refs, l_refs, rows, src_lens, offs, tile):
    n_src = len(src_lens)
    group = len(p_refs)
    n_s = len(s_refs)

    def scores(t):
        qt = q_ref[0, 0, rows(t), :]
        for si in range(n_src):
            s_refs[t % n_s][:, offs[si]:offs[si] + src_lens[si]] = _dot_nt(qt, kv_refs[2 * si][0, 0])

    def softmax(t):
        for r in range(0, tile, MLA_SM_ROWS):
            rs = slice(r, r + MLA_SM_ROWS)
            sb = s_refs[t % n_s][rs, :]
            p = jnp.exp2(sb - jnp.max(sb, axis=1, keepdims=True))
            l_refs[t][rs, :] = jnp.broadcast_to(jnp.sum(p, axis=1, keepdims=True), (MLA_SM_ROWS, 128))
            p_refs[t][rs, :] = p.astype(BF16)

    def values(t):
        acc = None
        for si in range(n_src):
            part = _dot(p_refs[t][:, offs[si]:offs[si] + src_lens[si]], kv_refs[2 * si + 1][0, 0])
            acc = part if acc is None else acc + part
        o_ref[0, rows(t), :] = acc / l_refs[t][...]

    scores(0)
    for t in range(group):
        if t + 1 < group:
            scores(t + 1)
        softmax(t)
        values(t)


def _mla(q, kv_sources):
    b, hq, tq_all, dq = q.shape
    tile = min(MLA_Q_TILE, tq_all // 2)
    tq = min(MLA_STEP_ROWS, tq_all)
    group = min(MLA_GROUP, tq // tile)
    n_s = min(MLA_SCORE_SLOTS, group)
    assert tq_all % tq == 0 and tq % (group * tile) == 0
    in_specs = [pl.BlockSpec((1, 1, tq, dq), lambda i, h, j: (i, h, j, 0))]
    args = [q]
    src_lens = []
    for k, v in kv_sources:
        n = k.shape[2]
        src_lens.append(n)
        in_specs.append(pl.BlockSpec((1, 1, n, dq), lambda i, h, j: (i, h, 0, 0)))
        in_specs.append(pl.BlockSpec((1, 1, n, MLA_V), lambda i, h, j: (i, h, 0, 0)))
        args += [k, v]
    n_keys = sum(src_lens)
    return pl.pallas_call(
        functools.partial(_mla_kernel, src_lens=tuple(src_lens), tile=tile, n_s=n_s),
        grid=(b, hq, tq_all // tq),
        in_specs=in_specs,
        out_specs=pl.BlockSpec((1, tq, MLA_V), lambda i, h, j: (i, j, h)),
        out_shape=jax.ShapeDtypeStruct((b, tq_all, hq * MLA_V), F32),
        scratch_shapes=(
            [pltpu.VMEM((tile, n_keys), F32)] * n_s
            + [pltpu.VMEM((tile, n_keys), BF16)] * group
            + [pltpu.VMEM((tile, 128), F32)] * group),
        compiler_params=_cparams(("arbitrary", "arbitrary", "arbitrary")),
        name="mla_%d" % len(kv_sources),
    )(*args)


def _swa_kernel(*refs, s_len, nblk):
    latent = s_len > 0
    if latent:
        q_ref, kl_ref, vl_ref, kc_ref, vc_ref, sink_ref, o_ref = refs
    else:
        q_ref, kc_ref, vc_ref, sink_ref, o_ref = refs
    kc = kc_ref[0]
    vc = vc_ref[0]
    rows = SWA_GROUP * SWA_BLOCK
    win = 3 * SWA_BLOCK
    chains = [(blk, kh) for blk in range(nblk) for kh in range(SWA_KV_HEADS)]
    kcat, vcat, valid = {}, {}, {}
    n_win_tiles = win // 128 if latent else 0
    if latent:
        d = (lax.broadcasted_iota(jnp.int32, (rows, win), 1)
             - (lax.broadcasted_iota(jnp.int32, (rows, win), 0) & (SWA_BLOCK - 1)))
        for blk in range(nblk):
            n = pl.program_id(1) * nblk + blk
            start = jnp.clip((n - 1) * SWA_BLOCK, 0, s_len - win)
            start = pl.multiple_of(start, SWA_BLOCK)
            kcat[blk] = jnp.concatenate([kl_ref[0, pl.ds(start, win), :], kc], axis=0)
            vcat[blk] = jnp.concatenate([vl_ref[0, pl.ds(start, win), :], vc], axis=0)
            valid[blk] = jnp.abs(d + (start - n * SWA_BLOCK)) <= SWA_WINDOW
    else:
        for blk in range(nblk):
            kcat[blk], vcat[blk] = kc, vc

    s, snk, mx = {}, {}, {}
    for c in chains:
        blk, kh = c
        heads = range(kh * SWA_GROUP, (kh + 1) * SWA_GROUP)
        lo, hi = kh * SWA_DH, (kh + 1) * SWA_DH
        q = q_ref[0, blk * SWA_BLOCK:(blk + 1) * SWA_BLOCK, :]
        qs = jnp.concatenate([q[:, h * SWA_DH:(h + 1) * SWA_DH] for h in heads], axis=0)
        snk[c] = jnp.concatenate(
            [jnp.broadcast_to(sink_ref[h:h + 1, 0:1] * LOG2E, (SWA_BLOCK, 1)) for h in heads], axis=0)
        sc = _dot_nt(qs, kcat[blk][:, lo:hi])
        tiles = [sc[:, j * 128:(j + 1) * 128] for j in range(sc.shape[1] // 128)]
        for j in range(n_win_tiles):
            tiles[j] = jnp.where(valid[blk][:, j * 128:(j + 1) * 128], tiles[j], NEG_INF)
        s[c] = tiles
    for c in chains:
        mm = functools.reduce(jnp.maximum, s[c])
        mx[c] = jnp.maximum(jnp.max(mm, axis=1, keepdims=True), snk[c])
    prob, den = {}, {}
    for c in chains:
        ps = [jnp.exp2(t - mx[c]) for t in s[c]]
        den[c] = jnp.sum(functools.reduce(jnp.add, ps), axis=1, keepdims=True) + jnp.exp2(snk[c] - mx[c])
        prob[c] = jnp.concatenate([p.astype(BF16) for p in ps], axis=1)
    for c in chains:
        blk, kh = c
        lo, hi = kh * SWA_DH, (kh + 1) * SWA_DH
        o = _dot(prob[c], vcat[blk][:, lo:hi]) / den[c]
        for g in range(SWA_GROUP):
            h = kh * SWA_GROUP + g
            o_ref[0, blk * SWA_BLOCK:(blk + 1) * SWA_BLOCK, h * SWA_DH:(h + 1) * SWA_DH] = (
                o[g * SWA_BLOCK:(g + 1) * SWA_BLOCK])


def _swa(q, k_lat, v_lat, k_ctx, v_ctx, params, l):
    sink_b = params["sink"]
    b, tq_all, dq = q.shape
    dkv = k_ctx.shape[2]
    n_ctx = k_ctx.shape[1]
    latent = k_lat is not None
    nblk = min(SWA_STEP_BLOCKS, tq_all // SWA_BLOCK)
    tq = nblk * SWA_BLOCK
    in_specs = [pl.BlockSpec((1, tq, dq), lambda i, j: (i, j, 0))]
    args = [q]
    if latent:
        s_len = k_lat.shape[1]
        in_specs += [pl.BlockSpec((1, s_len, dkv), lambda i, j: (i, 0, 0))] * 2
        args += [k_lat, v_lat]
    else:
        s_len = 0
    in_specs += [pl.BlockSpec((1, n_ctx, dkv), lambda i, j: (i, 0, 0))] * 2
    in_specs += [_layer_spec(sink_b, l)]
    args += [k_ctx, v_ctx, sink_b]
    return pl.pallas_call(
        functools.partial(_swa_kernel, s_len=s_len, nblk=nblk),
        grid=(b, tq_all // tq),
        in_specs=in_specs,
        out_specs=pl.BlockSpec((1, tq, dq), lambda i, j: (i, j, 0)),
        out_shape=jax.ShapeDtypeStruct((b, tq_all, dq), F32),
        compiler_params=_cparams(("arbitrary", "arbitrary")),
        name="swa_lat" if latent else "swa_ctx",
    )(*args)


CONV_ROWS = 64


def _conv_rows(prev_ref, cur_ref, next_ref, w_ref, b_ref, lg_ref, lb_ref, ext_ref, sh_ref, nt):
    j = pl.program_id(1)
    tm = cur_ref.shape[1]
    hl = CONV_HALO
    ext_ref[0:hl] = jnp.where(j > 0, prev_ref[0], 0.0)
    ext_ref[hl:hl + tm] = cur_ref[0]
    ext_ref[hl + tm:2 * hl + tm] = jnp.where(j < nt - 1, next_ref[0], 0.0)
    n_sh = sh_ref.shape[1]
    for sb in range(8):
        sh_ref[sb] = ext_ref[sb:sb + n_sh, :]
    off = hl - CONV_K // 2
    for r0 in range(0, tm, CONV_ROWS):
        acc = None
        for k in range(CONV_K):
            sb, a8 = (off + k) % 8, (off + k) // 8 * 8
            tap = sh_ref[sb, r0 + a8:r0 + a8 + CONV_ROWS, :].reshape(CONV_ROWS // 8, 8, -1)
            term = (tap * w_ref[k * 8:(k + 1) * 8, :][None]).reshape(CONV_ROWS, -1)
            acc = term if acc is None else acc + term
        hcv = acc + b_ref[...]
        mu = jnp.mean(hcv, axis=-1, keepdims=True)
        xc = hcv - mu
        y = xc * lax.rsqrt(jnp.mean(xc * xc, axis=-1, keepdims=True) + EPS) * lg_ref[...] + lb_ref[...]
        yield r0, y * jax.nn.sigmoid(y)


def _merge_kernel(oa_ref, ob_ref, gp_ref, gc_ref, gn_ref, cw_ref, cb_ref, lg_ref, lb_ref,
                  on_ref, wout_ref, x_ref, mod_ref, gm_ref, x1_ref, h2_ref, ext_ref, sh_ref, yc_ref, *, nt):
    on = on_ref[...]
    m = mod_ref[...]
    na = oa_ref.shape[2]
    nb = ob_ref.shape[2]
    tm = x_ref.shape[1]
    half = min(MERGE_HALF, tm)
    conv = _conv_rows(gp_ref, gc_ref, gn_ref, cw_ref, cb_ref, lg_ref, lb_ref, ext_ref, sh_ref, nt)
    for h0 in range(0, tm, half):
        rows = slice(h0, h0 + half)
        for _ in range(half // CONV_ROWS):
            r0, oc = next(conv)
            yc_ref[r0:r0 + CONV_ROWS, :] = _rms(oc, on[:, na + nb:]).astype(BF16)
        yab = jnp.concatenate([_rms(oa_ref[0, rows, :], on[:, 0:na]), _rms(ob_ref[0, rows, :], on[:, na:na + nb])],
                              axis=1).astype(BF16)
        proj = _dot(yab, wout_ref[0:na + nb, :]) + _dot(yc_ref[rows, :], wout_ref[na + nb:, :])
        x1 = x_ref[0, rows, :] + m[2:3] * proj
        x1_ref[0, rows, :] = x1
        h2_ref[0, rows, :] = (_rms(x1, gm_ref[...]) * (1.0 + m[4:5]) + m[3:4]).astype(BF16)


def _merge(oa, ob, glu, x, mod, mod_row, params, l):
    b, t, d = x.shape
    ch = glu.shape[2]
    tm = min(MERGE_ROWS, t)
    nt = t // tm
    hb = tm // CONV_HALO
    row = lambda i, j: (i, j, 0)
    stacks = [params[k] for k in ("conv_w", "conv_b", "conv_ln_g", "conv_ln_b", "out_norm", "w_out")]
    g_mlp = params["norm_mlp"]
    return pl.pallas_call(
        functools.partial(_merge_kernel, nt=nt),
        grid=(b, nt),
        in_specs=[
            pl.BlockSpec((1, tm, oa.shape[2]), row),
            pl.BlockSpec((1, tm, ob.shape[2]), row),
            pl.BlockSpec((1, CONV_HALO, ch), lambda i, j: (i, jnp.maximum(j * hb - 1, 0), 0)),
            pl.BlockSpec((1, tm, ch), row),
            pl.BlockSpec((1, CONV_HALO, ch), lambda i, j: (i, jnp.minimum((j + 1) * hb, nt * hb - 1), 0)),
        ] + [_layer_spec(a, l) for a in stacks] + [
            pl.BlockSpec((1, tm, d), row),
            _mod_spec(mod, l, mod_row),
            _layer_spec(g_mlp, l),
        ],
        out_specs=(pl.BlockSpec((1, tm, d), row), pl.BlockSpec((1, tm, d), row)),
        out_shape=(jax.ShapeDtypeStruct((b, t, d), F32), jax.ShapeDtypeStruct((b, t, d), BF16)),
        scratch_shapes=[pltpu.VMEM((tm + 2 * CONV_HALO, ch), F32),
                        pltpu.VMEM((8, tm + 2 * CONV_HALO - 8, ch), F32),
                        pltpu.VMEM((tm, ch), BF16)],
        compiler_params=_cparams(("arbitrary", "arbitrary")),
        name="merge",
    )(oa, ob, glu, glu, glu, *stacks, x, mod, g_mlp)


def _mlp_kernel(*refs, nf, final):
    if final:
        h_ref, w1_ref, w2_ref, x_ref, mod_ref, fn_ref, o_ref, acc_ref = refs
    else:
        h_ref, w1_ref, w2_ref, x_ref, mod_ref, o_ref, acc_ref = refs
    j = pl.program_id(2)

    @pl.when(j == 0)
    def _():
        acc_ref[...] = jnp.zeros_like(acc_ref)

    a = jnp.square(jnp.maximum(_dot(h_ref[0], w1_ref[...]), 0.0))
    acc_ref[...] += _dot(a.astype(BF16), w2_ref[...])

    @pl.when(j == nf - 1)
    def _():
        out = x_ref[0] + mod_ref[5:6, :] * acc_ref[...]
        if final:
            out = _rms(out, fn_ref[...])
        o_ref[0] = out


def _mlp(h2, x1, mod, mod_row, params, l, final_norm=None):
    b, t, d = x1.shape
    w1, w2 = params["w1"], params["w2"]
    dff = w1.shape[2]
    tr = min(MLP_ROWS, t)
    tf = MLP_FF_TILE
    nf = dff // tf
    final = final_norm is not None
    row = lambda i, r, j: (i, r, 0)
    in_specs = [
        pl.BlockSpec((1, tr, d), row),
        pl.BlockSpec((None, d, tf), lambda i, r, j: (l, 0, j)),
        pl.BlockSpec((None, tf, d), lambda i, r, j: (l, j, 0)),
        pl.BlockSpec((1, tr, d), row),
        _mod_spec(mod, l, mod_row),
    ]
    args = [h2, w1, w2, x1, mod]
    if final:
        in_specs.append(pl.BlockSpec(final_norm.shape, lambda i, r, j: (0, 0)))
        args.append(final_norm)
    return pl.pallas_call(
        functools.partial(_mlp_kernel, nf=nf, final=final),
        grid=(b, t // tr, nf),
        in_specs=in_specs,
        out_specs=pl.BlockSpec((1, tr, d), row),
        out_shape=jax.ShapeDtypeStruct((b, t, d), F32),
        scratch_shapes=[pltpu.VMEM((tr, d), F32)],
        compiler_params=_cparams(("arbitrary", "arbitrary", "arbitrary")),
        name="mlp_final" if final else "mlp",
    )(*args)


def _rope_tables(n_tok):
    rows = n_tok // GRID_W
    row = jnp.repeat(jnp.arange(rows, dtype=F32), GRID_W)
    col = jnp.tile(jnp.arange(GRID_W, dtype=F32), rows)
    n_freq = MLA_ROPE // 4
    inv_freq = ROPE_THETA ** (-jnp.arange(n_freq, dtype=F32) / n_freq)
    ar = row[:, None] * inv_freq
    ac = col[:, None] * inv_freq
    cos = jnp.concatenate([jnp.cos(ar), jnp.cos(ar), jnp.cos(ac), jnp.cos(ac)], axis=1)
    sin = jnp.concatenate([-jnp.sin(ar), jnp.sin(ar), -jnp.sin(ac), jnp.sin(ac)], axis=1)
    return jnp.tile(cos, (1, 2)), jnp.tile(sin, (1, 2))


def kernel(x, c, ctx, c_ctx, ada_w, ada_b, norm_mix, norm_mlp, w_in, mla_q_norm, mla_w_uq, mla_kv_norm, mla_w_ukv, swa_sink, conv_w, conv_b, conv_ln_g, conv_ln_b, out_norm, w_out, mlp_w1, mlp_w2, final_norm):
    b, s, d = x.shape
    n_ctx = ctx.shape[1]
    depth = ada_w.shape[0]
    assert s % (2 * MLA_Q_TILE) == 0 and s % MLP_ROWS == 0 and n_ctx % ROW_TILE == 0
    assert s % (SWA_STEP_BLOCKS * SWA_BLOCK) == 0
    assert s >= 3 * SWA_BLOCK and b + 1 <= 8

    cvec = jnp.concatenate([c, c_ctx[None, :], jnp.zeros((8 - b - 1, d), F32)], axis=0)
    mod = _ada(cvec, ada_w, ada_b).reshape(depth, 8, 6, d)
    rope_tabs = _rope_tables(s)

    vec = lambda v: v.reshape(depth, 1, -1)
    params = {
        "norm_mix": vec(norm_mix),
        "w_in": jnp.pad(w_in.astype(BF16), ((0, 0), (0, 0), (0, IN_PAD_WIDTH - w_in.shape[2]))),
        "q_norm": vec(mla_q_norm),
        "w_uq": jnp.pad(mla_w_uq.astype(BF16), ((0, 0), (0, 0), (0, 0), (0, MLA_QK_PAD - MLA_NOPE - MLA_ROPE))
                        ).reshape(depth, MLA_RANK, MLA_HEADS * MLA_QK_PAD),
        "kv_norm": vec(mla_kv_norm),
        "w_ukv": mla_w_ukv.astype(BF16).reshape(depth, MLA_RANK, MLA_HEADS * (MLA_NOPE + MLA_V)),
        "sink": jnp.broadcast_to(swa_sink[:, :, None], (depth, SWA_HEADS, 128)),
        "conv_w": jnp.broadcast_to(conv_w.reshape(depth, CONV_K, 1, CONV_CH), (depth, CONV_K, 8, CONV_CH)
                                   ).reshape(depth, CONV_K * 8, CONV_CH),
        "conv_b": vec(conv_b),
        "conv_ln_g": vec(conv_ln_g),
        "conv_ln_b": vec(conv_ln_b),
        "out_norm": vec(out_norm),
        "w_out": w_out.astype(BF16),
        "norm_mlp": vec(norm_mlp),
        "w1": mlp_w1.astype(BF16),
        "w2": mlp_w2.astype(BF16),
    }

    xc = ctx
    for l in range(depth):
        update_ctx = l < depth - 1
        qa, ka, va, bq, bk, bv, glu = _premix(x, mod, None, params, l, rope_tabs)
        qa_c, ka_c, va_c, bq_c, bk_c, bv_c, glu_c = _premix(xc, mod, b, params, l, None)

        oa = _mla(qa, [(ka, va), (ka_c, va_c)])
        ob = _swa(bq, bk, bv, bk_c, bv_c, params, l)
        x1, h2 = _merge(oa, ob, glu, x, mod, None, params, l)
        x = _mlp(h2, x1, mod, None, params, l, None if update_ctx else final_norm.reshape(1, d))

        if update_ctx:
            oa_c = _mla(qa_c, [(ka_c, va_c)])
            ob_c = _swa(bq_c, None, None, bk_c, bv_c, params, l)
            xc1, h2c = _merge(oa_c, ob_c, glu_c, xc, mod, b, params, l)
            flat = lambda a: a.reshape(1, b * n_ctx, d)
            xc = _mlp(flat(h2c), flat(xc1), mod, b, params, l).reshape(b, n_ctx, d)
    return x
```

```python
import functools

import jax
import jax.numpy as jnp
from jax import lax
from jax.experimental import pallas as pl
from jax.experimental.pallas import tpu as pltpu

F32 = jnp.float32
BF16 = jnp.bfloat16

EPS = 1e-6
NEG_INF = -1e30
GRID_W = 64
ROPE_THETA = 10000.0

MLA_HEADS = 8
MLA_RANK = 512
MLA_NOPE = 128
MLA_ROPE = 64
MLA_V = 128
MLA_QK_PAD = 256
LOG2E = 1.4426950408889634
MLA_Q_SCALE = (MLA_NOPE + MLA_ROPE) ** -0.5 * LOG2E
SWA_HEADS = 8
SWA_KV_HEADS = 2
SWA_GROUP = SWA_HEADS // SWA_KV_HEADS
SWA_DH = 64
SWA_WINDOW = 128
SWA_BLOCK = 128
SWA_Q_SCALE = SWA_DH ** -0.5 * LOG2E
CONV_CH = 512
CONV_K = 31
CONV_HALO = 16

OFF_AQ = 0
OFF_AKV = 512
OFF_KR = 1024
IN_PAD_WIDTH = 2944
TAIL_BQ = 0
TAIL_BK = 512
TAIL_BV = 640
TAIL_C = 768

ROW_TILE = 256
MERGE_ROWS = 512
MERGE_HALF = 256
PREMIX_ROWS = 512
MLP_ROWS = 512
MLP_FF_TILE = 1024
MLA_Q_TILE = 256
MLA_GROUP = 8
MLA_SCORE_SLOTS = 2
MLA_STEP_ROWS = 4096
MLA_SM_ROWS = 16
SWA_STEP_BLOCKS = 8
ADA_N_TILE = 1024
VMEM_LIMIT = 56 * 1024 * 1024


def _cparams(sem):
    return pltpu.CompilerParams(dimension_semantics=sem, vmem_limit_bytes=VMEM_LIMIT)


def _layer_spec(stack, l):
    _, a, b = stack.shape
    return pl.BlockSpec((None, a, b), lambda *_: (l, 0, 0), pipeline_mode=pl.Buffered(1))


def _mod_spec(mod, l, row):
    d = mod.shape[-1]
    return pl.BlockSpec((None, None, 6, d), lambda i, *_: (l, i if row is None else row, 0, 0))


def _rms(x, g):
    return x * lax.rsqrt(jnp.mean(x * x, axis=-1, keepdims=True) + EPS) * g


def _dot(a, b):
    return jnp.dot(a, b, preferred_element_type=F32)


def _dot_nt(a, b):
    return lax.dot_general(a, b, (((1,), (1,)), ((), ())), preferred_element_type=F32)


def _ada_kernel(c_ref, w_ref, b_ref, o_ref):
    c = c_ref[...]
    s = c * jax.nn.sigmoid(c)
    o_ref[0] = _dot(s.astype(BF16), w_ref[0].astype(BF16)) + b_ref[0]


def _ada(cvec, ada_w, ada_b):
    n_layers, d, n = ada_w.shape
    rows = cvec.shape[0]
    tn = ADA_N_TILE
    return pl.pallas_call(
        _ada_kernel,
        grid=(n_layers, n // tn),
        in_specs=[
            pl.BlockSpec((rows, d), lambda l, j: (0, 0)),
            pl.BlockSpec((1, d, tn), lambda l, j: (l, 0, j)),
            pl.BlockSpec((1, 1, tn), lambda l, j: (l, 0, j)),
        ],
        out_specs=pl.BlockSpec((1, rows, tn), lambda l, j: (l, 0, j)),
        out_shape=jax.ShapeDtypeStruct((n_layers, rows, n), F32),
        compiler_params=_cparams(("arbitrary", "arbitrary")),
        name="ada",
    )(cvec, ada_w, ada_b.reshape(n_layers, 1, n))


def _rope128(t, cos, sin):
    lane = lax.broadcasted_iota(jnp.int32, t.shape, 1)
    up = pltpu.roll(t, 128 - 16, 1)
    dn = pltpu.roll(t, 16, 1)
    sw = jnp.where((lane & 16) == 0, up, dn)
    return t * cos + sw * sin


def _premix_kernel(*refs, use_rope):
    if use_rope:
        (x_ref, mod_ref, g_ref, win_ref, qn_ref, wuq_ref, kvn_ref, wukv_ref, cos_ref, sin_ref,
         qa_ref, ka_ref, va_ref, bq_ref, bk_ref, bv_ref, glu_ref) = refs
        cos = cos_ref[...]
        sin = sin_ref[...]
        rope = lambda t: _rope128(t, cos, sin)
    else:
        (x_ref, mod_ref, g_ref, win_ref, qn_ref, wuq_ref, kvn_ref, wukv_ref,
         qa_ref, ka_ref, va_ref, bq_ref, bk_ref, bv_ref, glu_ref) = refs
        rope = lambda t: t

    m = mod_ref[...]
    h = _rms(x_ref[0], g_ref[...]) * (1.0 + m[1:2]) + m[0:1]
    p = _dot(h.astype(BF16), win_ref[...])
    tail = pltpu.roll(p[:, OFF_KR:], IN_PAD_WIDTH - OFF_KR - MLA_ROPE, 1)
    lane = lax.broadcasted_iota(jnp.int32, (p.shape[0], 128), 1)
    kr = jnp.where(lane < MLA_ROPE, p[:, OFF_KR:OFF_KR + 128], 0.0)

    qn = _rms(p[:, OFF_AQ:OFF_AQ + MLA_RANK], qn_ref[...])
    q = _dot(qn.astype(BF16), wuq_ref[...])
    for hd in range(MLA_HEADS):
        c0 = hd * MLA_QK_PAD
        qa_ref[0, hd, :, 0:128] = (q[:, c0:c0 + 128] * MLA_Q_SCALE).astype(BF16)
        qa_ref[0, hd, :, 128:256] = (rope(q[:, c0 + 128:c0 + 256]) * MLA_Q_SCALE).astype(BF16)

    kvn = _rms(p[:, OFF_AKV:OFF_AKV + MLA_RANK], kvn_ref[...])
    kv = _dot(kvn.astype(BF16), wukv_ref[...])
    kpe = rope(kr).astype(BF16)
    for hd in range(MLA_HEADS):
        c0 = hd * (MLA_NOPE + MLA_V)
        ka_ref[0, hd, :, 0:128] = kv[:, c0:c0 + 128].astype(BF16)
        ka_ref[0, hd, :, 128:256] = kpe
        va_ref[0, hd] = kv[:, c0 + 128:c0 + 256].astype(BF16)

    for t in range(SWA_HEADS * SWA_DH // 128):
        c0 = TAIL_BQ + t * 128
        bq_ref[0, :, t * 128:(t + 1) * 128] = (rope(tail[:, c0:c0 + 128]) * SWA_Q_SCALE).astype(BF16)
    bk_ref[0] = rope(tail[:, TAIL_BK:TAIL_BK + 128]).astype(BF16)
    bv_ref[0] = tail[:, TAIL_BV:TAIL_BV + 128].astype(BF16)

    glu_ref[0] = (tail[:, TAIL_C:TAIL_C + CONV_CH]
                  * jax.nn.sigmoid(tail[:, TAIL_C + CONV_CH:TAIL_C + 2 * CONV_CH]))


def _premix(x, mod, mod_row, params, l, rope_tabs):
    b, t, d = x.shape
    tm = min(PREMIX_ROWS, t)
    use_rope = rope_tabs is not None
    stacks = [params[k] for k in ("norm_mix", "w_in", "q_norm", "w_uq", "kv_norm", "w_ukv")]
    in_specs = [pl.BlockSpec((1, tm, d), lambda i, j: (i, j, 0)), _mod_spec(mod, l, mod_row)]
    in_specs += [_layer_spec(a, l) for a in stacks]
    args = [x, mod] + stacks
    if use_rope:
        in_specs += [pl.BlockSpec((tm, 128), lambda i, j: (j, 0))] * 2
        args += list(rope_tabs)
    hq = MLA_HEADS
    out_shape = (
        jax.ShapeDtypeStruct((b, hq, t, MLA_QK_PAD), BF16),
        jax.ShapeDtypeStruct((b, hq, t, MLA_QK_PAD), BF16),
        jax.ShapeDtypeStruct((b, hq, t, MLA_V), BF16),
        jax.ShapeDtypeStruct((b, t, SWA_HEADS * SWA_DH), BF16),
        jax.ShapeDtypeStruct((b, t, SWA_KV_HEADS * SWA_DH), BF16),
        jax.ShapeDtypeStruct((b, t, SWA_KV_HEADS * SWA_DH), BF16),
        jax.ShapeDtypeStruct((b, t, CONV_CH), F32),
    )
    out_specs = (
        pl.BlockSpec((1, hq, tm, MLA_QK_PAD), lambda i, j: (i, 0, j, 0)),
        pl.BlockSpec((1, hq, tm, MLA_QK_PAD), lambda i, j: (i, 0, j, 0)),
        pl.BlockSpec((1, hq, tm, MLA_V), lambda i, j: (i, 0, j, 0)),
        pl.BlockSpec((1, tm, SWA_HEADS * SWA_DH), lambda i, j: (i, j, 0)),
        pl.BlockSpec((1, tm, SWA_KV_HEADS * SWA_DH), lambda i, j: (i, j, 0)),
        pl.BlockSpec((1, tm, SWA_KV_HEADS * SWA_DH), lambda i, j: (i, j, 0)),
        pl.BlockSpec((1, tm, CONV_CH), lambda i, j: (i, j, 0)),
    )
    return pl.pallas_call(
        functools.partial(_premix_kernel, use_rope=use_rope),
        grid=(b, t // tm),
        in_specs=in_specs,
        out_specs=out_specs,
        out_shape=out_shape,
        compiler_params=_cparams(("arbitrary", "arbitrary")),
        name="premix_rope" if use_rope else "premix",
    )(*args)


def _mla_kernel(*refs, src_lens, tile, n_s):
    n_src = len(src_lens)
    q_ref = refs[0]
    kv_refs = refs[1:1 + 2 * n_src]
    o_ref = refs[1 + 2 * n_src]
    scratch = refs[2 + 2 * n_src:]
    group = (len(scratch) - n_s) // 2
    s_refs, p_refs, l_refs = scratch[:n_s], scratch[n_s:n_s + group], scratch[n_s + group:]
    offs = [sum(src_lens[:i]) for i in range(n_src)]
    n_groups = q_ref.shape[2] // (group * tile)

    def one_group(gi):
        def rows(t):
            start = (group * gi + t) * tile
            return pl.ds(start if isinstance(gi, int) else pl.multiple_of(start, tile), tile)

        _mla_group(q_ref, kv_refs, o_ref, s_refs, p_refs, l_refs, rows, src_lens, offs, tile)

    if n_groups == 1:
        one_group(0)
    else:
        pl.loop(0, n_groups)(one_group)


def _mla_group(q_ref, kv_refs, o_ref, s_refs, p_refs, l_refs, rows, src_lens, offs, tile):
    n_src = len(src_lens)
    group = len(p_refs)
    n_s = len(s_refs)

    def scores(t):
        qt = q_ref[0, 0, rows(t), :]
        for si in range(n_src):
            s_refs[t % n_s][:, offs[si]:offs[si] + src_lens[si]] = _dot_nt(qt, kv_refs[2 * si][0, 0])

    def softmax(t):
        for r in range(0, tile, MLA_SM_ROWS):
            rs = slice(r, r + MLA_SM_ROWS)
            sb = s_refs[t % n_s][rs, :]
            p = jnp.exp2(sb - jnp.max(sb, axis=1, keepdims=True))
            l_refs[t][rs, :] = jnp.broadcast_to(jnp.sum(p, axis=1, keepdims=True), (MLA_SM_ROWS, 128))
            p_refs[t][rs, :] = p.astype(BF16)

    def values(t):
        acc = None
        for si in range(n_src):
            part = _dot(p_refs[t][:, offs[si]:offs[si] + src_lens[si]], kv_refs[2 * si + 1][0, 0])
            acc = part if acc is None else acc + part
        o_ref[0, rows(t), :] = acc / l_refs[t][...]

    scores(0)
    for t in range(group):
        if t + 1 < group:
            scores(t + 1)
        softmax(t)
        values(t)


def _mla(q, kv_sources):
    b, hq, tq_all, dq = q.shape
    tile = min(MLA_Q_TILE, tq_all // 2)
    tq = min(MLA_STEP_ROWS, tq_all)
    group = min(MLA_GROUP, tq // tile)
    n_s = min(MLA_SCORE_SLOTS, group)
    assert tq_all % tq == 0 and tq % (group * tile) == 0
    in_specs = [pl.BlockSpec((1, 1, tq, dq), lambda i, h, j: (i, h, j, 0))]
    args = [q]
    src_lens = []
    for k, v in kv_sources:
        n = k.shape[2]
        src_lens.append(n)
        in_specs.append(pl.BlockSpec((1, 1, n, dq), lambda i, h, j: (i, h, 0, 0)))
        in_specs.append(pl.BlockSpec((1, 1, n, MLA_V), lambda i, h, j: (i, h, 0, 0)))
        args += [k, v]
    n_keys = sum(src_lens)
    return pl.pallas_call(
        functools.partial(_mla_kernel, src_lens=tuple(src_lens), tile=tile, n_s=n_s),
        grid=(b, hq, tq_all // tq),
        in_specs=in_specs,
        out_specs=pl.BlockSpec((1, tq, MLA_V), lambda i, h, j: (i, j, h)),
        out_shape=jax.ShapeDtypeStruct((b, tq_all, hq * MLA_V), F32),
        scratch_shapes=(
            [pltpu.VMEM((tile, n_keys), F32)] * n_s
            + [pltpu.VMEM((tile, n_keys), BF16)] * group
            + [pltpu.VMEM((tile, 128), F32)] * group),
        compiler_params=_cparams(("arbitrary", "arbitrary", "arbitrary")),
        name="mla_%d" % len(kv_sources),
    )(*args)


def _swa_kernel(*refs, s_len, nblk):
    latent = s_len > 0
    if latent:
        q_ref, kl_ref, vl_ref, kc_ref, vc_ref, sink_ref, o_ref = refs
    else:
        q_ref, kc_ref, vc_ref, sink_ref, o_ref = refs
    kc = kc_ref[0]
    vc = vc_ref[0]
    rows = SWA_GROUP * SWA_BLOCK
    win = 3 * SWA_BLOCK
    chains = [(blk, kh) for blk in range(nblk) for kh in range(SWA_KV_HEADS)]
    kcat, vcat, valid = {}, {}, {}
    n_win_tiles = win // 128 if latent else 0
    if latent:
        d = (lax.broadcasted_iota(jnp.int32, (rows, win), 1)
             - (lax.broadcasted_iota(jnp.int32, (rows, win), 0) & (SWA_BLOCK - 1)))
        for blk in range(nblk):
            n = pl.program_id(1) * nblk + blk
            start = jnp.clip((n - 1) * SWA_BLOCK, 0, s_len - win)
            start = pl.multiple_of(start, SWA_BLOCK)
            kcat[blk] = jnp.concatenate([kl_ref[0, pl.ds(start, win), :], kc], axis=0)
            vcat[blk] = jnp.concatenate([vl_ref[0, pl.ds(start, win), :], vc], axis=0)
            valid[blk] = jnp.abs(d + (start - n * SWA_BLOCK)) <= SWA_WINDOW
    else:
        for blk in range(nblk):
            kcat[blk], vcat[blk] = kc, vc

    s, snk, mx = {}, {}, {}
    for c in chains:
        blk, kh = c
        heads = range(kh * SWA_GROUP, (kh + 1) * SWA_GROUP)
        lo, hi = kh * SWA_DH, (kh + 1) * SWA_DH
        q = q_ref[0, blk * SWA_BLOCK:(blk + 1) * SWA_BLOCK, :]
        qs = jnp.concatenate([q[:, h * SWA_DH:(h + 1) * SWA_DH] for h in heads], axis=0)
        snk[c] = jnp.concatenate(
            [jnp.broadcast_to(sink_ref[h:h + 1, 0:1] * LOG2E, (SWA_BLOCK, 1)) for h in heads], axis=0)
        sc = _dot_nt(qs, kcat[blk][:, lo:hi])
        tiles = [sc[:, j * 128:(j + 1) * 128] for j in range(sc.shape[1] // 128)]
        for j in range(n_win_tiles):
            tiles[j] = jnp.where(valid[blk][:, j * 128:(j + 1) * 128], tiles[j], NEG_INF)
        s[c] = tiles
    for c in chains:
        mm = functools.reduce(jnp.maximum, s[c])
        mx[c] = jnp.maximum(jnp.max(mm, axis=1, keepdims=True), snk[c])
    prob, den = {}, {}
    for c in chains:
        ps = [jnp.exp2(t - mx[c]) for t in s[c]]
        den[c] = jnp.sum(functools.reduce(jnp.add, ps), axis=1, keepdims=True) + jnp.exp2(snk[c] - mx[c])
        prob[c] = jnp.concatenate([p.astype(BF16) for p in ps], axis=1)
    for c in chains:
        blk, kh = c
        lo, hi = kh * SWA_DH, (kh + 1) * SWA_DH
        o = _dot(prob[c], vcat[blk][:, lo:hi]) / den[c]
        for g in range(SWA_GROUP):
            h = kh * SWA_GROUP + g
            o_ref[0, blk * SWA_BLOCK:(blk + 1) * SWA_BLOCK, h * SWA_DH:(h + 1) * SWA_DH] = (
                o[g * SWA_BLOCK:(g + 1) * SWA_BLOCK])


def _swa(q, k_lat, v_lat, k_ctx, v_ctx, params, l):
    sink_b = params["sink"]
    b, tq_all, dq = q.shape
    dkv = k_ctx.shape[2]
    n_ctx = k_ctx.shape[1]
    latent = k_lat is not None
    nblk = min(SWA_STEP_BLOCKS, tq_all // SWA_BLOCK)
    tq = nblk * SWA_BLOCK
    in_specs = [pl.BlockSpec((1, tq, dq), lambda i, j: (i, j, 0))]
    args = [q]
    if latent:
        s_len = k_lat.shape[1]
        in_specs += [pl.BlockSpec((1, s_len, dkv), lambda i, j: (i, 0, 0))] * 2
        args += [k_lat, v_lat]
    else:
        s_len = 0
    in_specs += [pl.BlockSpec((1, n_ctx, dkv), lambda i, j: (i, 0, 0))] * 2
    in_specs += [_layer_spec(sink_b, l)]
    args += [k_ctx, v_ctx, sink_b]
    return pl.pallas_call(
        functools.partial(_swa_kernel, s_len=s_len, nblk=nblk),
        grid=(b, tq_all // tq),
        in_specs=in_specs,
        out_specs=pl.BlockSpec((1, tq, dq), lambda i, j: (i, j, 0)),
        out_shape=jax.ShapeDtypeStruct((b, tq_all, dq), F32),
        compiler_params=_cparams(("arbitrary", "arbitrary")),
        name="swa_lat" if latent else "swa_ctx",
    )(*args)


CONV_ROWS = 64


def _conv_rows(prev_ref, cur_ref, next_ref, w_ref, b_ref, lg_ref, lb_ref, ext_ref, sh_ref, nt):
    j = pl.program_id(1)
    tm = cur_ref.shape[1]
    hl = CONV_HALO
    ext_ref[0:hl] = jnp.where(j > 0, prev_ref[0], 0.0)
    ext_ref[hl:hl + tm] = cur_ref[0]
    ext_ref[hl + tm:2 * hl + tm] = jnp.where(j < nt - 1, next_ref[0], 0.0)
    n_sh = sh_ref.shape[1]
    for sb in range(8):
        sh_ref[sb] = ext_ref[sb:sb + n_sh, :]
    off = hl - CONV_K // 2
    for r0 in range(0, tm, CONV_ROWS):
        acc = None
        for k in range(CONV_K):
            sb, a8 = (off + k) % 8, (off + k) // 8 * 8
            tap = sh_ref[sb, r0 + a8:r0 + a8 + CONV_ROWS, :].reshape(CONV_ROWS // 8, 8, -1)
            term = (tap * w_ref[k * 8:(k + 1) * 8, :][None]).reshape(CONV_ROWS, -1)
            acc = term if acc is None else acc + term
        hcv = acc + b_ref[...]
        mu = jnp.mean(hcv, axis=-1, keepdims=True)
        xc = hcv - mu
        y = xc * lax.rsqrt(jnp.mean(xc * xc, axis=-1, keepdims=True) + EPS) * lg_ref[...] + lb_ref[...]
        yield r0, y * jax.nn.sigmoid(y)


def _merge_kernel(oa_ref, ob_ref, gp_ref, gc_ref, gn_ref, cw_ref, cb_ref, lg_ref, lb_ref,
                  on_ref, wout_ref, x_ref, mod_ref, gm_ref, x1_ref, h2_ref, ext_ref, sh_ref, yc_ref, *, nt):
    on = on_ref[...]
    m = mod_ref[...]
    na = oa_ref.shape[2]
    nb = ob_ref.shape[2]
    tm = x_ref.shape[1]
    half = min(MERGE_HALF, tm)
    conv = _conv_rows(gp_ref, gc_ref, gn_ref, cw_ref, cb_ref, lg_ref, lb_ref, ext_ref, sh_ref, nt)
    for h0 in range(0, tm, half):
        rows = slice(h0, h0 + half)
        for _ in range(half // CONV_ROWS):
            r0, oc = next(conv)
            yc_ref[r0:r0 + CONV_ROWS, :] = _rms(oc, on[:, na + nb:]).astype(BF16)
        yab = jnp.concatenate([_rms(oa_ref[0, rows, :], on[:, 0:na]), _rms(ob_ref[0, rows, :], on[:, na:na + nb])],
                              axis=1).astype(BF16)
        proj = _dot(yab, wout_ref[0:na + nb, :]) + _dot(yc_ref[rows, :], wout_ref[na + nb:, :])
        x1 = x_ref[0, rows, :] + m[2:3] * proj
        x1_ref[0, rows, :] = x1
        h2_ref[0, rows, :] = (_rms(x1, gm_ref[...]) * (1.0 + m[4:5]) + m[3:4]).astype(BF16)


def _merge(oa, ob, glu, x, mod, mod_row, params, l):
    b, t, d = x.shape
    ch = glu.shape[2]
    tm = min(MERGE_ROWS, t)
    nt = t // tm
    hb = tm // CONV_HALO
    row = lambda i, j: (i, j, 0)
    stacks = [params[k] for k in ("conv_w", "conv_b", "conv_ln_g", "conv_ln_b", "out_norm", "w_out")]
    g_mlp = params["norm_mlp"]
    return pl.pallas_call(
        functools.partial(_merge_kernel, nt=nt),
        grid=(b, nt),
        in_specs=[
            pl.BlockSpec((1, tm, oa.shape[2]), row),
            pl.BlockSpec((1, tm, ob.shape[2]), row),
            pl.BlockSpec((1, CONV_HALO, ch), lambda i, j: (i, jnp.maximum(j * hb - 1, 0), 0)),
            pl.BlockSpec((1, tm, ch), row),
            pl.BlockSpec((1, CONV_HALO, ch), lambda i, j: (i, jnp.minimum((j + 1) * hb, nt * hb - 1), 0)),
        ] + [_layer_spec(a, l) for a in stacks] + [
            pl.BlockSpec((1, tm, d), row),
            _mod_spec(mod, l, mod_row),
            _layer_spec(g_mlp, l),
        ],
        out_specs=(pl.BlockSpec((1, tm, d), row), pl.BlockSpec((1, tm, d), row)),
        out_shape=(jax.ShapeDtypeStruct((b, t, d), F32), jax.ShapeDtypeStruct((b, t, d), BF16)),
        scratch_shapes=[pltpu.VMEM((tm + 2 * CONV_HALO, ch), F32),
                        pltpu.VMEM((8, tm + 2 * CONV_HALO - 8, ch), F32),
                        pltpu.VMEM((tm, ch), BF16)],
        compiler_params=_cparams(("arbitrary", "arbitrary")),
        name="merge",
    )(oa, ob, glu, glu, glu, *stacks, x, mod, g_mlp)


def _mlp_kernel(*refs, nf, final):
    if final:
        h_ref, w1_ref, w2_ref, x_ref, mod_ref, fn_ref, o_ref, acc_ref = refs
    else:
        h_ref, w1_ref, w2_ref, x_ref, mod_ref, o_ref, acc_ref = refs
    j = pl.program_id(2)

    @pl.when(j == 0)
    def _():
        acc_ref[...] = jnp.zeros_like(acc_ref)

    a = jnp.square(jnp.maximum(_dot(h_ref[0], w1_ref[...]), 0.0))
    acc_ref[...] += _dot(a.astype(BF16), w2_ref[...])

    @pl.when(j == nf - 1)
    def _():
        out = x_ref[0] + mod_ref[5:6, :] * acc_ref[...]
        if final:
            out = _rms(out, fn_ref[...])
        o_ref[0] = out


def _mlp(h2, x1, mod, mod_row, params, l, final_norm=None):
    b, t, d = x1.shape
    w1, w2 = params["w1"], params["w2"]
    dff = w1.shape[2]
    tr = min(MLP_ROWS, t)
    tf = MLP_FF_TILE
    nf = dff // tf
    final = final_norm is not None
    row = lambda i, r, j: (i, r, 0)
    in_specs = [
        pl.BlockSpec((1, tr, d), row),
        pl.BlockSpec((None, d, tf), lambda i, r, j: (l, 0, j)),
        pl.BlockSpec((None, tf, d), lambda i, r, j: (l, j, 0)),
        pl.BlockSpec((1, tr, d), row),
        _mod_spec(mod, l, mod_row),
    ]
    args = [h2, w1, w2, x1, mod]
    if final:
        in_specs.append(pl.BlockSpec(final_norm.shape, lambda i, r, j: (0, 0)))
        args.append(final_norm)
    return pl.pallas_call(
        functools.partial(_mlp_kernel, nf=nf, final=final),
        grid=(b, t // tr, nf),
        in_specs=in_specs,
        out_specs=pl.BlockSpec((1, tr, d), row),
        out_shape=jax.ShapeDtypeStruct((b, t, d), F32),
        scratch_shapes=[pltpu.VMEM((tr, d), F32)],
        compiler_params=_cparams(("arbitrary", "arbitrary", "arbitrary")),
        name="mlp_final" if final else "mlp",
    )(*args)


def _rope_tables(n_tok):
    rows = n_tok // GRID_W
    row = jnp.repeat(jnp.arange(rows, dtype=F32), GRID_W)
    col = jnp.tile(jnp.arange(GRID_W, dtype=F32), rows)
    n_freq = MLA_ROPE // 4
    inv_freq = ROPE_THETA ** (-jnp.arange(n_freq, dtype=F32) / n_freq)
    ar = row[:, None] * inv_freq
    ac = col[:, None] * inv_freq
    cos = jnp.concatenate([jnp.cos(ar), jnp.cos(ar), jnp.cos(ac), jnp.cos(ac)], axis=1)
    sin = jnp.concatenate([-jnp.sin(ar), jnp.sin(ar), -jnp.sin(ac), jnp.sin(ac)], axis=1)
    return jnp.tile(cos, (1, 2)), jnp.tile(sin, (1, 2))


def kernel(x, c, ctx, c_ctx, ada_w, ada_b, norm_mix, norm_mlp, w_in, mla_q_norm, mla_w_uq, mla_kv_norm, mla_w_ukv, swa_sink, conv_w, conv_b, conv_ln_g, conv_ln_b, out_norm, w_out, mlp_w1, mlp_w2, final_norm):
    b, s, d = x.shape
    n_ctx = ctx.shape[1]
    depth = ada_w.shape[0]
    assert s % (2 * MLA_Q_TILE) == 0 and s % MLP_ROWS == 0 and n_ctx % ROW_TILE == 0
    assert s % (SWA_STEP_BLOCKS * SWA_BLOCK) == 0
    assert s >= 3 * SWA_BLOCK and b + 1 <= 8

    cvec = jnp.concatenate([c, c_ctx[None, :], jnp.zeros((8 - b - 1, d), F32)], axis=0)
    mod = _ada(cvec, ada_w, ada_b).reshape(depth, 8, 6, d)
    rope_tabs = _rope_tables(s)

    vec = lambda v: v.reshape(depth, 1, -1)
    params = {
        "norm_mix": vec(norm_mix),
        "w_in": jnp.pad(w_in.astype(BF16), ((0, 0), (0, 0), (0, IN_PAD_WIDTH - w_in.shape[2]))),
        "q_norm": vec(mla_q_norm),
        "w_uq": jnp.pad(mla_w_uq.astype(BF16), ((0, 0), (0, 0), (0, 0), (0, MLA_QK_PAD - MLA_NOPE - MLA_ROPE))
                        ).reshape(depth, MLA_RANK, MLA_HEADS * MLA_QK_PAD),
        "kv_norm": vec(mla_kv_norm),
        "w_ukv": mla_w_ukv.astype(BF16).reshape(depth, MLA_RANK, MLA_HEADS * (MLA_NOPE + MLA_V)),
        "sink": jnp.broadcast_to(swa_sink[:, :, None], (depth, SWA_HEADS, 128)),
        "conv_w": jnp.broadcast_to(conv_w.reshape(depth, CONV_K, 1, CONV_CH), (depth, CONV_K, 8, CONV_CH)
                                   ).reshape(depth, CONV_K * 8, CONV_CH),
        "conv_b": vec(conv_b),
        "conv_ln_g": vec(conv_ln_g),
        "conv_ln_b": vec(conv_ln_b),
        "out_norm": vec(out_norm),
        "w_out": w_out.astype(BF16),
        "norm_mlp": vec(norm_mlp),
        "w1": mlp_w1.astype(BF16),
        "w2": mlp_w2.astype(BF16),
    }

    xc = ctx
    for l in range(depth):
        update_ctx = l < depth - 1
        qa, ka, va, bq, bk, bv, glu = _premix(x, mod, None, params, l, rope_tabs)
        qa_c, ka_c, va_c, bq_c, bk_c, bv_c, glu_c = _premix(xc, mod, b, params, l, None)

        oa = _mla(qa, [(ka, va), (ka_c, va_c)])
        ob = _swa(bq, bk, bv, bk_c, bv_c, params, l)
        x1, h2 = _merge(oa, ob, glu, x, mod, None, params, l)
        x = _mlp(h2, x1, mod, None, params, l, None if update_ctx else final_norm.reshape(1, d))

        if update_ctx:
            oa_c = _mla(qa_c, [(ka_c, va_c)])
            ob_c = _swa(bq_c, None, None, bk_c, bv_c, params, l)
            xc1, h2c = _merge(oa_c, ob_c, glu_c, xc, mod, b, params, l)
            flat = lambda a: a.reshape(1, b * n_ctx, d)
            xc = _mlp(flat(h2c), flat(xc1), mod, b, params, l).reshape(b, n_ctx, d)
    return x
```
